```python
import math
import jax
import jax.numpy as jnp
from jax import lax
import numpy as np

D_MODEL = 4096
BATCH = 4
SEQ = 2048
DEPTH = 2
DEC_BATCH = 32
DEC_SEQ = 8
PAST_LEN = 16384
PAGE_SIZE = 128

N_HEADS = 64
HEAD_DIM = 64
Q_WIDTH = N_HEADS * HEAD_DIM
SCALE = HEAD_DIM ** -0.5
A_KV_HEADS = 4
A_GROUP = N_HEADS // A_KV_HEADS
A_KV_WIDTH = 2 * A_KV_HEADS * HEAD_DIM
A_IN_WIDTH = Q_WIDTH + 3 * A_KV_WIDTH + 3 * N_HEADS
CMP_BLOCK = 32
CMP_STRIDE = 16
CMP_RATIO = CMP_BLOCK // CMP_STRIDE
CMP_HIDDEN = 256
SEL_BLOCK = 64
SEL_PER_CMP = SEL_BLOCK // CMP_STRIDE
N_SELECT = 16
N_LOCAL = 2
A_WINDOW = 512
SEL_Q_BLOCK = 64
B_KV_HEADS = 8
B_GROUP = N_HEADS // B_KV_HEADS
B_WINDOW = 128
BAND_BLOCK = 128
REL_BUCKETS = 32
REL_MAX_DIST = 128
D_FF = 11008
N_EXPERTS = 8
TOP_K = 2
D_FF_EXPERT = 14336
N_A = DEPTH // 2
N_B = DEPTH - N_A
N_DENSE = (DEPTH + 1) // 2
N_MOE = DEPTH // 2
ALPHA = (2 * DEPTH) ** 0.25
BETA = (8 * DEPTH) ** -0.25
LN_EPS = 1e-5
NEG_INF = -1e30
TINY = 1e-30
FORCE = 1e9

kernel_name = 'yoco_nsa_swa_sink_decoder_step'


def layer_norm(x, g, b):
    xf = x.astype(jnp.float32)
    mu = jnp.mean(xf, axis=-1, keepdims=True)
    var = jnp.mean(jnp.square(xf - mu), axis=-1, keepdims=True)
    return ((xf - mu) * lax.rsqrt(var + LN_EPS) * g.astype(jnp.float32) + b.astype(jnp.float32)).astype(x.dtype)


def rel_bucket(dist):
    n = jnp.maximum(dist, 0)
    exact = REL_BUCKETS // 2
    nf = jnp.maximum(n, exact).astype(jnp.float32)
    large = exact + (jnp.log(nf / exact) / math.log(REL_MAX_DIST / exact) * (REL_BUCKETS - exact)).astype(jnp.int32)
    return jnp.where(n < exact, n, jnp.minimum(large, REL_BUCKETS - 1))


def masked_softmax(s, mask, sink=None):
    s = jnp.where(mask, s, NEG_INF)
    m = jnp.max(s, axis=-1, keepdims=True)
    if sink is not None:
        m = jnp.maximum(m, sink)
    e = jnp.where(mask, jnp.exp(s - m), 0.0)
    den = jnp.sum(e, axis=-1, keepdims=True)
    if sink is not None:
        den = den + jnp.exp(sink - m)
    return e / jnp.maximum(den, TINY)


def attend_window(q, kv, tpos, kpos, window, rel_table, sink):
    T, N = tpos.shape[0], kpos.shape[0]
    kvh, grp = q.shape[2], q.shape[3]
    dist = tpos[:, None] - kpos[None, :]
    mask = (dist >= 0) & (dist <= window) & (kpos[None, :] >= 0)
    bias = rel_table[rel_bucket(dist)].reshape(T, N, kvh, grp).transpose(2, 3, 0, 1).astype(jnp.float32)
    s = jnp.einsum('btkgd,bnkd->bkgtn', q, kv[:, :, 0]).astype(jnp.float32) + bias
    p = masked_softmax(s, mask, sink)
    return jnp.einsum('bkgtn,bnkd->btkgd', p.astype(q.dtype), kv[:, :, 1])


def banded_attend(q, kv, window, rel_table, sink):
    B, S = q.shape[:2]
    span = window + BAND_BLOCK
    kv_pad = jnp.pad(kv, ((0, 0), (window, 0), (0, 0), (0, 0), (0, 0)))

    def block(i):
        start = i * BAND_BLOCK
        qb = lax.dynamic_slice_in_dim(q, start, BAND_BLOCK, axis=1)
        kvb = lax.dynamic_slice_in_dim(kv_pad, start, span, axis=1)
        tpos = start + jnp.arange(BAND_BLOCK, dtype=jnp.int32)
        kpos = start - window + jnp.arange(span, dtype=jnp.int32)
        return attend_window(qb, kvb, tpos, kpos, window, rel_table, sink)

    o = lax.map(block, jnp.arange(S // BAND_BLOCK, dtype=jnp.int32))
    return jnp.moveaxis(o, 0, 1).reshape(q.shape)


def dense_swiglu(x, w_in, w_out):
    gate, up = jnp.split(x @ w_in, 2, axis=-1)
    return (jax.nn.silu(gate) * up) @ w_out


def moe_swiglu(x, w_router, b_router, w_in, w_out, m):
    shp = x.shape
    xt = x.reshape(-1, shp[-1])
    logits = (xt @ w_router + b_router).astype(jnp.float32)
    top_val, top_idx = lax.top_k(logits, TOP_K)
    gate = jax.nn.softmax(top_val, axis=-1)
    dense_gate = jnp.sum(jax.nn.one_hot(top_idx, N_EXPERTS, dtype=jnp.float32) * gate[..., None], axis=1).astype(x.dtype)
    out = jnp.zeros_like(xt)
    for e in range(N_EXPERTS):
        out = out + dense_gate[:, e:e + 1] * dense_swiglu(xt, w_in[m, e], w_out[m, e])
    return out.reshape(shp)


def nsa_project(x, w_in):
    B, T, _ = x.shape
    h = x @ w_in
    q = h[..., :Q_WIDTH].reshape(B, T, A_KV_HEADS, A_GROUP, HEAD_DIM) * SCALE
    kv_c = h[..., Q_WIDTH:Q_WIDTH + A_KV_WIDTH].reshape(B, T, 2, A_KV_HEADS, HEAD_DIM)
    kv_s = h[..., Q_WIDTH + A_KV_WIDTH:Q_WIDTH + 2 * A_KV_WIDTH].reshape(B, T, 2, A_KV_HEADS, HEAD_DIM)
    kv_w = h[..., Q_WIDTH + 2 * A_KV_WIDTH:Q_WIDTH + 3 * A_KV_WIDTH].reshape(B, T, 2, A_KV_HEADS, HEAD_DIM)
    gates = jax.nn.sigmoid(h[..., Q_WIDTH + 3 * A_KV_WIDTH:].reshape(B, T, 3, A_KV_HEADS, A_GROUP))
    return q, kv_c, kv_s, kv_w, gates


def halfblock_features(kv, w1):
    B, N = kv.shape[:2]
    hb = kv.reshape(B, N // CMP_STRIDE, CMP_STRIDE, 2, A_KV_HEADS, HEAD_DIM)
    w1r = w1.reshape(2, CMP_RATIO, CMP_STRIDE, HEAD_DIM, CMP_HIDDEN)
    return jnp.einsum('bnpckd,crpdh->bnrckh', hb, w1r)


def compress_blocks(feat, pe, w1, b1, w2):
    n_cmp = feat.shape[1] - CMP_RATIO + 1
    pe_term = jnp.einsum('cpd,cpdh->ch', pe, w1) + b1
    pre = sum(feat[:, r:r + n_cmp, r] for r in range(CMP_RATIO)) + pe_term[:, None, :]
    return jnp.einsum('bnckh,chd->bnckd', jax.nn.silu(pre), w2)


def cmp_attend(q, ckv, tpos, rel_table):
    T = tpos.shape[0]
    n_cmp = ckv.shape[1]
    cend = jnp.arange(n_cmp, dtype=jnp.int32) * CMP_STRIDE + CMP_BLOCK - 1
    dist = tpos[:, None] - cend[None, :]
    mask = (dist >= 0)[None, :, None, None, :]
    bias = rel_table[rel_bucket(dist)].reshape(T, n_cmp, A_KV_HEADS, A_GROUP).transpose(0, 2, 3, 1).astype(jnp.float32)
    s = jnp.einsum('btkgd,bnkd->btkgn', q, ckv[:, :, 0]).astype(jnp.float32) + bias[None]
    p = masked_softmax(s, mask)
    o = jnp.einsum('btkgn,bnkd->btkgd', p.astype(q.dtype), ckv[:, :, 1])
    return o, p


def select_blocks(p_cmp, tpos, seq_len):
    n_sel = -(-seq_len // SEL_BLOCK)
    n_cmp = p_cmp.shape[-1]
    score = jnp.sum(p_cmp, axis=3)
    front = CMP_RATIO - 1
    back = SEL_PER_CMP * n_sel - n_cmp
    pp = jnp.pad(score, ((0, 0), (0, 0), (0, 0), (front, back)))
    slc = sum(pp[..., m - n + front::SEL_PER_CMP][..., :n_sel] for m in range(SEL_PER_CMP) for n in range(CMP_RATIO))
    blk = jnp.arange(n_sel, dtype=jnp.int32)
    lag = (tpos // SEL_BLOCK)[:, None] - blk[None, :]
    avail = blk[None, :] * SEL_BLOCK <= tpos[:, None]
    forced = (blk[None, :] == 0) | ((lag >= 0) & (lag < N_LOCAL))
    slc = jnp.where(forced[None, :, None, :], FORCE, slc)
    slc = jnp.where(avail[None, :, None, :], slc, -FORCE)
    top_val, ids = lax.top_k(slc, min(N_SELECT, n_sel))
    return ids, top_val > -0.5 * FORCE


def sel_attend(q, tpos, ids, valid, kv_g, rel_table):
    B, T, kvh, grp, hd = q.shape
    nk = ids.shape[-1] * SEL_BLOCK
    kpos = (ids[..., None] * SEL_BLOCK + jnp.arange(SEL_BLOCK, dtype=jnp.int32)).reshape(B, T, kvh, nk)
    dist = tpos[None, :, None, None] - kpos
    mask = ((dist >= 0) & jnp.repeat(valid, SEL_BLOCK, axis=-1))[:, :, :, None, :]
    kidx = jnp.arange(kvh)[None, None, :, None]
    bias = rel_table.reshape(REL_BUCKETS, kvh, grp)[rel_bucket(dist), kidx]
    k = kv_g[..., 0, :].reshape(B, T, kvh, nk, hd)
    v = kv_g[..., 1, :].reshape(B, T, kvh, nk, hd)
    s = jnp.einsum('btkgd,btknd->btkgn', q, k).astype(jnp.float32) + jnp.moveaxis(bias, -1, 3).astype(jnp.float32)
    p = masked_softmax(s, mask)
    return jnp.einsum('btkgn,btknd->btkgd', p.astype(q.dtype), v)


def nsa_combine(o_cmp, o_sel, o_win, gates, w_out):
    o = gates[:, :, 0, :, :, None] * o_cmp + gates[:, :, 1, :, :, None] * o_sel + gates[:, :, 2, :, :, None] * o_win
    B, T = o.shape[:2]
    return o.reshape(B, T, Q_WIDTH) @ w_out


def nsa_prompt(x, w_in, pe, w1, b1, w2, w_out, rel_table):
    B, S, _ = x.shape
    q, kv_c, kv_s, kv_w, gates = nsa_project(x, w_in)
    tpos = jnp.arange(S, dtype=jnp.int32)
    ckv = compress_blocks(halfblock_features(kv_c, w1), pe, w1, b1, w2)
    o_cmp, p_cmp = cmp_attend(q, ckv, tpos, rel_table)
    ids, valid = select_blocks(p_cmp, tpos, S)
    kv_blocks = kv_s.reshape(B, S // SEL_BLOCK, SEL_BLOCK, 2, A_KV_HEADS, HEAD_DIM)
    bidx = jnp.arange(B)[:, None, None, None]
    kvidx = jnp.arange(A_KV_HEADS)[None, None, :, None]

    def sel_chunk(i):
        st = i * SEL_Q_BLOCK
        ids_c = lax.dynamic_slice_in_dim(ids, st, SEL_Q_BLOCK, axis=1)
        val_c = lax.dynamic_slice_in_dim(valid, st, SEL_Q_BLOCK, axis=1)
        q_c = lax.dynamic_slice_in_dim(q, st, SEL_Q_BLOCK, axis=1)
        g = kv_blocks[bidx, ids_c, :, :, kvidx]
        return sel_attend(q_c, st + jnp.arange(SEL_Q_BLOCK, dtype=jnp.int32), ids_c, val_c, g, rel_table)

    o_sel = lax.map(sel_chunk, jnp.arange(S // SEL_Q_BLOCK, dtype=jnp.int32))
    o_sel = jnp.moveaxis(o_sel, 0, 1).reshape(q.shape)
    o_win = banded_attend(q, kv_w, A_WINDOW, rel_table, None)
    y = nsa_combine(o_cmp, o_sel, o_win, gates, w_out)
    return y, kv_c, kv_s, kv_w[:, S - min(A_WINDOW, S):]


def nsa_sample(x, cache_cmp_kv, cache_sel_kv, win_buf, page_table, a, w_in, pe, w1, b1, w2, w_out, rel_table):
    B, T, _ = x.shape
    q, kv_c, kv_s, kv_w, gates = nsa_project(x, w_in)
    tpos = PAST_LEN + jnp.arange(T, dtype=jnp.int32)
    seq_len = PAST_LEN + T
    n_phys = cache_cmp_kv.shape[1]
    cmp_pages = cache_cmp_kv.reshape(N_A * n_phys, PAGE_SIZE, 2, A_KV_HEADS, HEAD_DIM)
    past_c = cmp_pages[a * n_phys + page_table].reshape(B, PAST_LEN, 2, A_KV_HEADS, HEAD_DIM)
    n_new = (T // CMP_STRIDE) * CMP_STRIDE
    feat = jnp.concatenate([halfblock_features(past_c, w1), halfblock_features(kv_c[:, :n_new], w1)], axis=1)
    ckv = compress_blocks(feat, pe, w1, b1, w2)
    o_cmp, p_cmp = cmp_attend(q, ckv, tpos, rel_table)
    ids, valid = select_blocks(p_cmp, tpos, seq_len)
    bpp = PAGE_SIZE // SEL_BLOCK
    npb = PAST_LEN // SEL_BLOCK
    nnb = -(-seq_len // SEL_BLOCK) - npb
    bidx = jnp.arange(B)[:, None, None, None]
    kvidx = jnp.arange(A_KV_HEADS)[None, None, :, None]
    sel_pool = cache_sel_kv.reshape(N_A * n_phys * bpp, SEL_BLOCK, 2, A_KV_HEADS, HEAD_DIM)
    jp = jnp.minimum(ids, npb - 1)
    pid = a * n_phys * bpp + page_table[bidx, jp // bpp] * bpp + jp % bpp
    g_past = sel_pool[pid, :, :, kvidx]
    new_blocks = jnp.pad(kv_s, ((0, 0), (0, nnb * SEL_BLOCK - T), (0, 0), (0, 0), (0, 0))).reshape(B, nnb, SEL_BLOCK, 2, A_KV_HEADS, HEAD_DIM)
    g_new = new_blocks[bidx, jnp.clip(ids - npb, 0, nnb - 1), :, :, kvidx]
    g = jnp.where((ids >= npb)[..., None, None, None], g_new, g_past)
    o_sel = sel_attend(q, tpos, ids, valid, g, rel_table)
    win = jnp.concatenate([win_buf, kv_w], axis=1)
    wb = win_buf.shape[1]
    kpos = PAST_LEN - wb + jnp.arange(wb + T, dtype=jnp.int32)
    o_win = attend_window(q, win, tpos, kpos, A_WINDOW, rel_table, None)
    y = nsa_combine(o_cmp, o_sel, o_win, gates, w_out)
    n_keep = min(A_WINDOW, wb + T)
    return y, kv_c, kv_s, win[:, wb + T - n_keep:]


def shared_kv(x, w_kv):
    B, T, _ = x.shape
    return (x @ w_kv).reshape(B, T, 2, B_KV_HEADS, HEAD_DIM)


def sink_logits(sinks):
    return sinks.reshape(B_KV_HEADS, B_GROUP)[:, :, None, None].astype(jnp.float32)


def swa_prompt(x, kv, w_q, sinks, w_out, rel_table):
    B, S, _ = x.shape
    q = (x @ w_q).reshape(B, S, B_KV_HEADS, B_GROUP, HEAD_DIM) * SCALE
    o = banded_attend(q, kv, B_WINDOW, rel_table, sink_logits(sinks))
    return o.reshape(B, S, Q_WIDTH) @ w_out


def swa_sample(x, kv, w_q, sinks, w_out, rel_table):
    B, T, _ = x.shape
    N = kv.shape[1]
    q = (x @ w_q).reshape(B, T, B_KV_HEADS, B_GROUP, HEAD_DIM) * SCALE
    tpos = PAST_LEN + jnp.arange(T, dtype=jnp.int32)
    kpos = PAST_LEN + T - N + jnp.arange(N, dtype=jnp.int32)
    o = attend_window(q, kv, tpos, kpos, B_WINDOW, rel_table, sink_logits(sinks))
    return o.reshape(B, T, Q_WIDTH) @ w_out


def setup_inputs(seed: int = 0) -> dict:
    key = jax.random.key(seed)
    ks = jax.random.split(key, 32)

    def nrm(i, shape, scale):
        return jax.random.normal(ks[i], shape, jnp.float32) * scale

    n_pages = PAST_LEN // PAGE_SIZE
    n_used = DEC_BATCH * n_pages
    n_phys = (5 * n_used + 3) // 4
    wb_a = min(A_WINDOW, PAST_LEN)
    wb_b = min(B_WINDOW, PAST_LEN)
    page_table = jax.random.permutation(ks[0], n_phys)[:n_used].reshape(DEC_BATCH, n_pages).astype(jnp.int32)
    return {
        'x_prompt': nrm(1, (BATCH, SEQ, D_MODEL), 1.0),
        'x_sample': nrm(2, (DEC_BATCH, DEC_SEQ, D_MODEL), 1.0),
        'cache_cmp_kv': nrm(3, (N_A, n_phys, PAGE_SIZE, 2, A_KV_HEADS, HEAD_DIM), 1.0),
        'cache_sel_kv': nrm(4, (N_A, n_phys, PAGE_SIZE, 2, A_KV_HEADS, HEAD_DIM), 1.0),
        'state_win_kv': nrm(5, (N_A, DEC_BATCH, wb_a, 2, A_KV_HEADS, HEAD_DIM), 1.0),
        'state_shared_kv': nrm(6, (DEC_BATCH, wb_b, 2, B_KV_HEADS, HEAD_DIM), 1.0),
        'page_table': page_table,
        'rel_table': nrm(7, (REL_BUCKETS, N_HEADS), 0.5),
        'ln_g': 1.0 + nrm(8, (DEPTH, 2, D_MODEL), 0.05),
        'ln_b': nrm(9, (DEPTH, 2, D_MODEL), 0.02),
        'a_w_in': nrm(10, (N_A, D_MODEL, A_IN_WIDTH), D_MODEL ** -0.5),
        'a_cmp_pe': nrm(11, (N_A, 2, CMP_BLOCK, HEAD_DIM), 0.1),
        'a_cmp_w1': nrm(12, (N_A, 2, CMP_BLOCK, HEAD_DIM, CMP_HIDDEN), (CMP_BLOCK * HEAD_DIM) ** -0.5),
        'a_cmp_b1': nrm(13, (N_A, 2, CMP_HIDDEN), 0.01),
        'a_cmp_w2': nrm(14, (N_A, 2, CMP_HIDDEN, HEAD_DIM), CMP_HIDDEN ** -0.5),
        'a_w_out': nrm(15, (N_A, Q_WIDTH, D_MODEL), BETA * Q_WIDTH ** -0.5),
        'b_w_kv': nrm(16, (D_MODEL, 2 * B_KV_HEADS * HEAD_DIM), D_MODEL ** -0.5),
        'b_w_q': nrm(17, (N_B, D_MODEL, Q_WIDTH), D_MODEL ** -0.5),
        'b_sinks': nrm(18, (N_B, N_HEADS), 0.5),
        'b_w_out': nrm(19, (N_B, Q_WIDTH, D_MODEL), BETA * Q_WIDTH ** -0.5),
        'dense_w_in': nrm(20, (N_DENSE, D_MODEL, 2 * D_FF), D_MODEL ** -0.5),
        'dense_w_out': nrm(21, (N_DENSE, D_FF, D_MODEL), BETA * D_FF ** -0.5),
        'moe_router_w': nrm(22, (N_MOE, D_MODEL, N_EXPERTS), D_MODEL ** -0.5),
        'moe_router_b': nrm(23, (N_MOE, N_EXPERTS), 0.01),
        'moe_w_in': nrm(24, (N_MOE, N_EXPERTS, D_MODEL, 2 * D_FF_EXPERT), D_MODEL ** -0.5),
        'moe_w_out': nrm(25, (N_MOE, N_EXPERTS, D_FF_EXPERT, D_MODEL), BETA * D_FF_EXPERT ** -0.5),
    }


def reference(x_prompt, x_sample, cache_cmp_kv, cache_sel_kv, state_win_kv, state_shared_kv, page_table,
              rel_table, ln_g, ln_b, a_w_in, a_cmp_pe, a_cmp_w1, a_cmp_b1, a_cmp_w2, a_w_out,
              b_w_kv, b_w_q, b_sinks, b_w_out, dense_w_in, dense_w_out,
              moe_router_w, moe_router_b, moe_w_in, moe_w_out):
    xp, xs = x_prompt, x_sample
    cmp_p, cmp_s, sel_p, sel_s, win_p, win_s = [], [], [], [], [], []
    sh_p = None
    sh_s = None
    for layer in range(DEPTH):
        if layer < N_A:
            yp, cp, sp, wp = nsa_prompt(xp, a_w_in[layer], a_cmp_pe[layer], a_cmp_w1[layer], a_cmp_b1[layer],
                                        a_cmp_w2[layer], a_w_out[layer], rel_table)
            ys, cs, ss, ws = nsa_sample(xs, cache_cmp_kv, cache_sel_kv, state_win_kv[layer], page_table, layer,
                                        a_w_in[layer], a_cmp_pe[layer], a_cmp_w1[layer], a_cmp_b1[layer],
                                        a_cmp_w2[layer], a_w_out[layer], rel_table)
            cmp_p.append(cp)
            cmp_s.append(cs)
            sel_p.append(sp)
            sel_s.append(ss)
            win_p.append(wp)
            win_s.append(ws)
        else:
            bl = layer - N_A
            yp = swa_prompt(xp, sh_p, b_w_q[bl], b_sinks[bl], b_w_out[bl], rel_table)
            ys = swa_sample(xs, sh_s, b_w_q[bl], b_sinks[bl], b_w_out[bl], rel_table)
        xp = layer_norm(ALPHA * xp + yp, ln_g[layer, 0], ln_b[layer, 0])
        xs = layer_norm(ALPHA * xs + ys, ln_g[layer, 0], ln_b[layer, 0])
        if layer % 2 == 0:
            fp = dense_swiglu(xp, dense_w_in[layer // 2], dense_w_out[layer // 2])
            fs = dense_swiglu(xs, dense_w_in[layer // 2], dense_w_out[layer // 2])
        else:
            m = layer // 2
            fp = moe_swiglu(xp, moe_router_w[m], moe_router_b[m], moe_w_in, moe_w_out, m)
            fs = moe_swiglu(xs, moe_router_w[m], moe_router_b[m], moe_w_in, moe_w_out, m)
        xp = layer_norm(ALPHA * xp + fp, ln_g[layer, 1], ln_b[layer, 1])
        xs = layer_norm(ALPHA * xs + fs, ln_g[layer, 1], ln_b[layer, 1])
        if layer == N_A - 1:
            sh_p = shared_kv(xp, b_w_kv)
            sh_s = jnp.concatenate([state_shared_kv, shared_kv(xs, b_w_kv)], axis=1)
    s_len = sh_p.shape[1]
    n_len = sh_s.shape[1]
    new_shared_kv_prompt = sh_p[:, s_len - min(B_WINDOW, s_len):]
    new_shared_kv_sample = sh_s[:, n_len - min(B_WINDOW, n_len):]
    new_cmp_kv_prompt = jnp.stack(cmp_p)
    new_cmp_kv_sample = jnp.stack(cmp_s)
    new_sel_kv_prompt = jnp.stack(sel_p)
    new_sel_kv_sample = jnp.stack(sel_s)
    new_win_kv_prompt = jnp.stack(win_p)
    new_win_kv_sample = jnp.stack(win_s)
    return (xp, xs, new_cmp_kv_prompt, new_cmp_kv_sample, new_sel_kv_prompt, new_sel_kv_sample,
            new_win_kv_prompt, new_win_kv_sample, new_shared_kv_prompt, new_shared_kv_sample)
```

```python
import functools
import math

import jax
import jax.numpy as jnp
import numpy as np
from jax import lax
from jax.experimental import pallas as pl
from jax.experimental.pallas import tpu as pltpu

D_MODEL = 4096
BATCH = 4
SEQ = 2048
DEPTH = 2
DEC_BATCH = 32
DEC_SEQ = 8
PAST_LEN = 16384
PAGE_SIZE = 128
N_HEADS = 64
HEAD_DIM = 64
Q_WIDTH = N_HEADS * HEAD_DIM
SCALE = HEAD_DIM ** -0.5
A_KV_HEADS = 4
A_GROUP = N_HEADS // A_KV_HEADS
A_KV_WIDTH = 2 * A_KV_HEADS * HEAD_DIM
CMP_BLOCK = 32
CMP_STRIDE = 16
CMP_HIDDEN = 256
SEL_BLOCK = 64
N_SELECT = 16
N_LOCAL = 2
A_WINDOW = 512
B_KV_HEADS = 8
B_GROUP = N_HEADS // B_KV_HEADS
B_WINDOW = 128
REL_BUCKETS = 32
REL_MAX_DIST = 128
D_FF = 11008
N_EXPERTS = 8
TOP_K = 2
D_FF_EXPERT = 14336
ALPHA = (2 * DEPTH) ** 0.25
LN_EPS = 1e-5
NEG_INF = -1e30
TINY = 1e-30
FORCE = 1e9

LANES = 128
MP = BATCH * SEQ
MS = DEC_BATCH * DEC_SEQ
MT = MP + MS
TM = 1056
QT = 128
N_PAGES = PAST_LEN // PAGE_SIZE
VMEM_LIMIT = 56 * 1024 * 1024

F32 = jnp.float32
BF16 = jnp.bfloat16
HIGHEST = lax.Precision.HIGHEST


def _params(sem, vmem=VMEM_LIMIT):
    return pltpu.CompilerParams(dimension_semantics=sem, vmem_limit_bytes=vmem)


def _silu(x):
    return x * (1.0 / (1.0 + jnp.exp(-x)))


def _nt_dot(a, b):
    return lax.dot_general(a, b, (((1,), (1,)), ((), ())), preferred_element_type=F32)


def _tn_dot(a, b):
    return lax.dot_general(a, b, (((0,), (0,)), ((), ())), preferred_element_type=F32)


def _mm_body(x_ref, w_ref, o_ref, *, scale):
    acc = jnp.dot(x_ref[...], w_ref[...], preferred_element_type=F32)
    if scale != 1.0:
        acc = acc * scale
    o_ref[...] = acc.astype(o_ref.dtype)


def _matmul(x, w, *, tm, tn, out_dtype, scale=1.0, name):
    m, k = x.shape
    n = w.shape[1]
    return pl.pallas_call(
        functools.partial(_mm_body, scale=scale),
        grid=(n // tn, m // tm),
        in_specs=[pl.BlockSpec((tm, k), lambda j, i: (i, 0)),
                  pl.BlockSpec((k, tn), lambda j, i: (0, j))],
        out_specs=pl.BlockSpec((tm, tn), lambda j, i: (i, j)),
        out_shape=jax.ShapeDtypeStruct((m, n), out_dtype),
        compiler_params=_params(("arbitrary", "arbitrary")),
        name=name,
    )(x, w)


def _mm_swiglu_body(x_ref, wg_ref, wu_ref, o_ref):
    x = x_ref[...]
    g = jnp.dot(x, wg_ref[...], preferred_element_type=F32)
    u = jnp.dot(x, wu_ref[...], preferred_element_type=F32)
    o_ref[...] = (_silu(g) * u).astype(o_ref.dtype)


def _matmul_swiglu(x, w_in, *, tm, tn, name):
    m, k = x.shape
    f = w_in.shape[1] // 2
    nb = f // tn
    return pl.pallas_call(
        _mm_swiglu_body,
        grid=(m // tm, nb),
        in_specs=[pl.BlockSpec((tm, k), lambda i, j: (i, 0)),
                  pl.BlockSpec((k, tn), lambda i, j: (0, j)),
                  pl.BlockSpec((k, tn), lambda i, j: (0, j + nb))],
        out_specs=pl.BlockSpec((tm, tn), lambda i, j: (i, j)),
        out_shape=jax.ShapeDtypeStruct((m, f), BF16),
        compiler_params=_params(("arbitrary", "arbitrary")),
        name=name,
    )(x, w_in, w_in)


def _ln_body(*refs, alpha, n_add, gated):
    x_ref = refs[0]
    add_refs = refs[1:1 + n_add]
    pos = 1 + n_add
    gate_ref = refs[pos] if gated else None
    pos += 1 if gated else 0
    g_ref, b_ref, o_ref, ob_ref = refs[pos:pos + 4]
    z = alpha * x_ref[...]
    for a, r in enumerate(add_refs):
        y = r[...]
        if gated:
            y = y * gate_ref[:, a:a + 1]
        z = z + y
    mu = jnp.mean(z, axis=-1, keepdims=True)
    zc = z - mu
    var = jnp.mean(zc * zc, axis=-1, keepdims=True)
    out = zc * lax.rsqrt(var + LN_EPS) * g_ref[...] + b_ref[...]
    o_ref[...] = out
    ob_ref[...] = out.astype(BF16)


def _residual_ln(x, adds, add_maps, gate, g, b, *, tm, name):
    m, d = x.shape
    gated = gate is not None
    in_specs = [pl.BlockSpec((tm, d), lambda i: (i, 0))]
    args = [x]
    for a, mp in zip(adds, add_maps):
        in_specs.append(pl.BlockSpec((tm, d), mp))
        args.append(a)
    if gated:
        in_specs.append(pl.BlockSpec((tm, gate.shape[1]), lambda i: (i, 0)))
        args.append(gate)
    in_specs += [pl.BlockSpec((1, d), lambda i: (0, 0)), pl.BlockSpec((1, d), lambda i: (0, 0))]
    args += [g.reshape(1, d), b.reshape(1, d)]
    return pl.pallas_call(
        functools.partial(_ln_body, alpha=ALPHA, n_add=len(adds), gated=gated),
        grid=(m // tm,),
        in_specs=in_specs,
        out_specs=[pl.BlockSpec((tm, d), lambda i: (i, 0)), pl.BlockSpec((tm, d), lambda i: (i, 0))],
        out_shape=[jax.ShapeDtypeStruct((m, d), F32), jax.ShapeDtypeStruct((m, d), BF16)],
        compiler_params=_params(("arbitrary",)),
        name=name,
    )(*args)


def _bias_body(tab_ref, dist_ref, o_ref, *, lo, hi):
    h = pl.program_id(0)
    d = dist_ref[...]
    n = jnp.maximum(d, 0)
    exact = REL_BUCKETS // 2
    nf = jnp.maximum(n, exact).astype(F32)
    large = exact + (jnp.log(nf * (1.0 / exact)) / math.log(REL_MAX_DIST / exact) * (REL_BUCKETS - exact)).astype(jnp.int32)
    bucket = jnp.where(n < exact, n, jnp.minimum(large, REL_BUCKETS - 1))
    acc = jnp.zeros(d.shape, F32)
    for bk in range(REL_BUCKETS):
        acc = jnp.where(bucket == bk, tab_ref[bk, h], acc)
    valid = (d >= lo) & (d <= hi)
    o_ref[0] = jnp.where(valid, acc, NEG_INF)


def _bias_table(rel_table, dist, lo, hi, name):
    r, c = dist.shape
    return pl.pallas_call(
        functools.partial(_bias_body, lo=lo, hi=hi),
        grid=(N_HEADS,),
        in_specs=[pl.BlockSpec(memory_space=pltpu.SMEM),
                  pl.BlockSpec((r, c), lambda h: (0, 0))],
        out_specs=pl.BlockSpec((1, r, c), lambda h: (h, 0, 0)),
        out_shape=jax.ShapeDtypeStruct((N_HEADS, r, c), F32),
        compiler_params=_params(("arbitrary",)),
        name=name,
    )(rel_table, jnp.asarray(dist, jnp.int32))


BIG = 1 << 30


def _dist_tables():
    t128 = np.arange(QT)[:, None]
    tS = np.arange(SEQ)[:, None]
    t8 = PAST_LEN + np.arange(DEC_SEQ)[:, None]
    d = {}
    d["p_cmp"] = (tS - (np.arange(SEQ // CMP_STRIDE)[None, :] * CMP_STRIDE + CMP_BLOCK - 1), 0, BIG)
    d["p_win"] = (t128 + A_WINDOW - np.arange(A_WINDOW + QT)[None, :], 0, A_WINDOW)
    d["p_sel"] = (np.concatenate([u * QT + t128 - np.arange(QT)[None, :] for u in range(3)], axis=0), 0, BIG)
    d["p_swa"] = (t128 + B_WINDOW - np.arange(B_WINDOW + QT)[None, :], 0, B_WINDOW)
    n_cmp_s = PAST_LEN // CMP_STRIDE
    d["s_cmp"] = (t8 - (np.arange(n_cmp_s)[None, :] * CMP_STRIDE + CMP_BLOCK - 1), 0, BIG)
    d["s_win"] = (t8 - (PAST_LEN - A_WINDOW + np.arange(A_WINDOW + 16)[None, :]), 0, A_WINDOW)
    d["s_last"] = (t8 - (PAST_LEN - PAGE_SIZE + np.arange(PAGE_SIZE)[None, :]), 0, BIG)
    d["s_new"] = (t8 - (PAST_LEN + np.arange(16)[None, :]), 0, BIG)
    d["s_swa"] = (t8 - (PAST_LEN - B_WINDOW + np.arange(B_WINDOW + 16)[None, :]), 0, B_WINDOW)
    return d


CMP_PAIRS = CMP_BLOCK // 2


def _compress_body(pid_ref, pages_ref, wpair_ref, w2_ref, pe_ref, w1_ref, b1_ref, o_ref, buf, sem, *, pc, nch, npg):
    b = pl.program_id(0)
    c = pl.program_id(1)
    step = b * nch + c
    nsteps = pl.num_programs(0) * nch
    slot = step % 2
    hb = PAGE_SIZE // CMP_STRIDE
    m = pc * hb

    def copies(bb, cc, sl):
        out = []
        for i in range(pc):
            pg = pid_ref[bb, cc * pc + i]
            out.append(pltpu.make_async_copy(pages_ref.at[pg], buf.at[sl, pl.ds(i * hb, hb)], sem.at[sl]))
        pg = pid_ref[bb, jnp.minimum(cc * pc + pc, npg - 1)]
        out.append(pltpu.make_async_copy(pages_ref.at[pg, pl.ds(0, 1)], buf.at[sl, pl.ds(pc * hb, 1)], sem.at[sl]))
        return out

    @pl.when(step == 0)
    def _():
        for cp in copies(b, c, slot):
            cp.start()

    @pl.when(step + 1 < nsteps)
    def _():
        wrap = c + 1 == nch
        nb = jnp.where(wrap, b + 1, b)
        nc = jnp.where(wrap, 0, c + 1)
        for cp in copies(nb, nc, 1 - slot):
            cp.start()

    for cp in copies(b, c, slot):
        cp.wait()

    for kv in range(2):
        pe_term = jnp.dot(pe_ref[kv], w1_ref[kv], preferred_element_type=F32)[0:1] + b1_ref[kv]
        pe2 = jnp.concatenate([pe_term, pe_term], axis=1)
        acc = jnp.zeros((2 * m, 2 * CMP_HIDDEN), F32)
        for q in range(CMP_PAIRS):
            rows = []
            for j in range(2):
                lanes = pl.ds(kv * 256 + j * LANES, LANES)
                a0 = buf[slot, pl.ds((2 * q) // CMP_STRIDE, m), (2 * q) % CMP_STRIDE, lanes]
                a1 = buf[slot, pl.ds((2 * q + 1) // CMP_STRIDE, m), (2 * q + 1) % CMP_STRIDE, lanes]
                rows.append(jnp.concatenate([a0, a1], axis=1))
            lhs = jnp.concatenate(rows, axis=0).astype(BF16)
            acc = acc + jnp.dot(lhs, wpair_ref[kv, q], preferred_element_type=F32)
        hid = _silu(acc + pe2)
        out = jnp.dot(hid.astype(BF16), w2_ref[kv], preferred_element_type=F32)
        o_ref[0, kv, 0] = out[:m].astype(o_ref.dtype)
        o_ref[0, kv, 1] = out[m:].astype(o_ref.dtype)


def _compress(page_ids, pages, wpair, w2bd, pe8, w1f, b1, *, pc, name):
    nb, npg = page_ids.shape
    pages = pages.reshape(-1, PAGE_SIZE // CMP_STRIDE, CMP_STRIDE, A_KV_WIDTH)
    nch = npg // pc
    m = pc * (PAGE_SIZE // CMP_STRIDE)
    const = lambda nd: (lambda b, c, pid: (0,) * nd)
    grid_spec = pltpu.PrefetchScalarGridSpec(
        num_scalar_prefetch=1,
        grid=(nb, nch),
        in_specs=[pl.BlockSpec(memory_space=pl.ANY),
                  pl.BlockSpec(wpair.shape, const(4)),
                  pl.BlockSpec(w2bd.shape, const(3)),
                  pl.BlockSpec(pe8.shape, const(3)),
                  pl.BlockSpec(w1f.shape, const(3)),
                  pl.BlockSpec(b1.shape, const(3))],
        out_specs=pl.BlockSpec((1, 2, 2, m, LANES), lambda b, c, pid: (b, 0, 0, c, 0)),
        scratch_shapes=[pltpu.VMEM((2, m + 1, CMP_STRIDE, A_KV_WIDTH), F32),
                        pltpu.SemaphoreType.DMA((2,))],
    )
    return pl.pallas_call(
        functools.partial(_compress_body, pc=pc, nch=nch, npg=npg),
        grid_spec=grid_spec,
        out_shape=jax.ShapeDtypeStruct((nb, 2, 2, npg * (PAGE_SIZE // CMP_STRIDE), LANES), BF16),
        compiler_params=_params(("arbitrary", "arbitrary")),
        name=name,
    )(page_ids, pages, wpair, w2bd, pe8, w1f, b1)


def _select_mask(slc, tpos, blk, n_sel, top_n):
    avail = (blk * SEL_BLOCK <= tpos) & (blk < n_sel)
    lag = tpos // SEL_BLOCK - blk
    forced = (blk == 0) | ((lag >= 0) & (lag < N_LOCAL))
    slc = jnp.where(forced, FORCE, slc)
    slc = jnp.where(avail, slc, -FORCE)

    def one(jp, cnt):
        col = jnp.sum(jnp.where(blk == jp, slc, 0.0), axis=1, keepdims=True)
        ahead = (col > slc) | ((col == slc) & (blk > jp))
        return cnt + jnp.where(ahead, 1.0, 0.0)

    cnt = lax.fori_loop(0, n_sel, one, jnp.zeros(slc.shape, F32))
    return jnp.where(avail & (cnt < top_n), 1.0, 0.0)


def _overlap_matrix(n_cmp, n_sel, rows, cols):
    msel = np.zeros((rows, cols), np.float32)
    per = SEL_BLOCK // CMP_STRIDE
    for j in range(n_sel):
        for mm in range(per):
            for r in range(CMP_BLOCK // CMP_STRIDE):
                n = per * j + mm - r
                if 0 <= n < n_cmp:
                    msel[n, j] += 1.0
    return msel


def _nsa_prompt_body(q_ref, kc_ref, vc_ref, ks_ref, vs_ref, kw_ref, vw_ref, bc_ref, bw_ref, bt_ref,
                     msel_ref, eexp_ref, oc_ref, os_ref, ow_ref, m_scr, l_scr, acc_scr):
    i = pl.program_id(2)
    grp = A_GROUP
    rows = grp * QT
    qblk = q_ref[...]
    q = jnp.concatenate([qblk[:, g * HEAD_DIM:(g + 1) * HEAD_DIM] for g in range(grp)], axis=0)

    def to_block(o):
        return jnp.concatenate([o[g * QT:(g + 1) * QT] for g in range(grp)], axis=1)

    s = _nt_dot(q, kc_ref[0, 0]) + bc_ref[0].reshape(rows, LANES)
    valid = s > 0.5 * NEG_INF
    mx = jnp.max(s, axis=1, keepdims=True)
    e = jnp.where(valid, jnp.exp(s - mx), 0.0)
    p = e / jnp.maximum(jnp.sum(e, axis=1, keepdims=True), TINY)
    oc_ref[...] = to_block(jnp.dot(p.astype(BF16), vc_ref[0, 0], preferred_element_type=F32))

    score = jnp.sum(p.reshape(grp, QT, LANES), axis=0)
    slc = jnp.dot(score, msel_ref[...], precision=HIGHEST, preferred_element_type=F32)
    tpos = i * QT + lax.broadcasted_iota(jnp.int32, (QT, LANES), 0)
    blk = lax.broadcasted_iota(jnp.int32, (QT, LANES), 1)
    sel = _select_mask(slc, tpos, blk, SEQ // SEL_BLOCK, N_SELECT).astype(BF16)

    m_scr[...] = jnp.full(m_scr.shape, NEG_INF, F32)
    l_scr[...] = jnp.zeros(l_scr.shape, F32)
    acc_scr[...] = jnp.zeros(acc_scr.shape, F32)

    def sel_tile(jj, carry):
        start = pl.multiple_of(jj * QT, QT)
        k = ks_ref[0, 0, pl.ds(start, QT), :]
        v = vs_ref[0, 0, pl.ds(start, QT), :]
        u = jnp.minimum(i - jj, 2)
        st = _nt_dot(q, k) + bt_ref[0, u]
        keymask = jnp.dot(sel, eexp_ref[jj], preferred_element_type=F32)
        keymask = jnp.broadcast_to(keymask[None], (grp, QT, QT)).reshape(rows, QT)
        st = jnp.where(keymask > 0.5, st, NEG_INF)
        m_old = m_scr[...]
        m_new = jnp.maximum(m_old, jnp.max(st, axis=1, keepdims=True))
        a = jnp.exp(m_old - m_new)
        et = jnp.where(st > 0.5 * NEG_INF, jnp.exp(st - m_new), 0.0)
        l_scr[...] = a * l_scr[...] + jnp.sum(et, axis=1, keepdims=True)
        acc_scr[...] = a * acc_scr[...] + jnp.dot(et.astype(BF16), v, preferred_element_type=F32)
        m_scr[...] = m_new
        return carry

    lax.fori_loop(0, i + 1, sel_tile, 0)
    os_ref[...] = to_block(acc_scr[...] / jnp.maximum(l_scr[...], TINY))

    n_t = A_WINDOW // QT + 1
    bw = bw_ref[0].reshape(rows, n_t * QT)
    parts = []
    for u in range(n_t):
        j = i - (n_t - 1) + u
        start = pl.multiple_of(jnp.maximum(j, 0) * QT, QT)
        su = _nt_dot(q, kw_ref[0, 0, pl.ds(start, QT), :]) + bw[:, u * QT:(u + 1) * QT]
        parts.append(jnp.where(j >= 0, su, NEG_INF))
    sw = jnp.concatenate(parts, axis=1)
    mx = jnp.max(sw, axis=1, keepdims=True)
    ew = jnp.where(sw > 0.5 * NEG_INF, jnp.exp(sw - mx), 0.0)
    pw = (ew / jnp.maximum(jnp.sum(ew, axis=1, keepdims=True), TINY)).astype(BF16)
    ow = jnp.zeros((rows, HEAD_DIM), F32)
    for u in range(n_t):
        j = i - (n_t - 1) + u
        start = pl.multiple_of(jnp.maximum(j, 0) * QT, QT)
        ow = ow + jnp.dot(pw[:, u * QT:(u + 1) * QT], vw_ref[0, 0, pl.ds(start, QT), :], preferred_element_type=F32)
    ow_ref[...] = to_block(ow)


def _nsa_prompt(q, kc, vc, ks, vs, kw, vw, bias_cmp, bias_win, bias_sel):
    nq = SEQ // QT
    grp = A_GROUP
    n_cmp = SEQ // CMP_STRIDE
    msel = jnp.asarray(_overlap_matrix(n_cmp - 1, SEQ // SEL_BLOCK, n_cmp, LANES))
    eexp = np.zeros((nq, LANES, QT), np.float32)
    for jj in range(nq):
        for n in range(QT):
            eexp[jj, (jj * QT + n) // SEL_BLOCK, n] = 1.0
    eexp = jnp.asarray(eexp, BF16)
    kv_spec = lambda n: pl.BlockSpec((1, 1, n, HEAD_DIM), lambda b, k, i: (b, k, 0, 0))
    out_spec = pl.BlockSpec((QT, grp * HEAD_DIM), lambda b, k, i: (b * nq + i, k))
    n_win = A_WINDOW + QT
    return pl.pallas_call(
        _nsa_prompt_body,
        grid=(BATCH, A_KV_HEADS, nq),
        in_specs=[pl.BlockSpec((QT, grp * HEAD_DIM), lambda b, k, i: (b * nq + i, k)),
                  kv_spec(n_cmp), kv_spec(n_cmp), kv_spec(SEQ), kv_spec(SEQ), kv_spec(SEQ), kv_spec(SEQ),
                  pl.BlockSpec((1, grp, QT, LANES), lambda b, k, i: (k, 0, i, 0)),
                  pl.BlockSpec((1, grp, QT, n_win), lambda b, k, i: (k, 0, 0, 0)),
                  pl.BlockSpec((1, 3, grp * QT, QT), lambda b, k, i: (k, 0, 0, 0)),
                  pl.BlockSpec((n_cmp, LANES), lambda b, k, i: (0, 0)),
                  pl.BlockSpec((nq, LANES, QT), lambda b, k, i: (0, 0, 0))],
        out_specs=[out_spec, out_spec, out_spec],
        out_shape=[jax.ShapeDtypeStruct((MP, Q_WIDTH), F32)] * 3,
        scratch_shapes=[pltpu.VMEM((grp * QT, 1), F32), pltpu.VMEM((grp * QT, 1), F32),
                        pltpu.VMEM((grp * QT, HEAD_DIM), F32)],
        compiler_params=_params(("arbitrary", "arbitrary", "arbitrary")),
        name="nsa_prompt",
    )(q, kc, vc, ks, vs, kw, vw, bias_cmp, bias_win, bias_sel, msel, eexp)


SEL_PAGES = 16
S_COLS = A_KV_HEADS * A_GROUP * DEC_SEQ
S_NSEL = -(-(PAST_LEN + DEC_SEQ) // SEL_BLOCK)
S_NSEL_PAD = 384


def _softmax_cols(s):
    mx = jnp.max(s, axis=0, keepdims=True)
    e = jnp.where(s > 0.5 * NEG_INF, jnp.exp(s - mx), 0.0)
    return e / jnp.maximum(jnp.sum(e, axis=0, keepdims=True), TINY)


def _nsa_sample_body(pid_ref, qt_ref, kc_ref, vc_ref, kw_ref, vw_ref, kn_ref, vn_ref, bc_ref, bw_ref, bl_ref, bn_ref,
                     cf_ref, gsum_ref, msel_ref, gexp_ref, eexp_ref, pages_ref, o_ref,
                     buf, sem, sel_scr, m_scr, l_scr, acc_scr, *, npg):
    b = pl.program_id(0)
    c = pl.program_id(1)
    nch = npg // SEL_PAGES
    step = b * nch + c
    nsteps = pl.num_programs(0) * nch
    slot = step % 2
    kw = A_KV_HEADS * HEAD_DIM

    def copies(bb, cc, sl):
        return [pltpu.make_async_copy(pages_ref.at[pid_ref[bb, cc * SEL_PAGES + i]],
                                      buf.at[sl, pl.ds(i * PAGE_SIZE, PAGE_SIZE)], sem.at[sl])
                for i in range(SEL_PAGES)]

    @pl.when(step == 0)
    def _():
        for cp in copies(b, c, slot):
            cp.start()

    @pl.when(step + 1 < nsteps)
    def _():
        wrap = c + 1 == nch
        for cp in copies(jnp.where(wrap, b + 1, b), jnp.where(wrap, 0, c + 1), 1 - slot):
            cp.start()

    qt = qt_ref[0]

    @pl.when(c == 0)
    def _():
        p = _softmax_cols(jnp.dot(kc_ref[0], qt, preferred_element_type=F32) + bc_ref[...])
        o_ref[0, 0] = _tn_dot(vc_ref[0], p.astype(BF16))
        score = lax.dot_general(gsum_ref[...], p, (((1,), (1,)), ((), ())), precision=HIGHEST,
                                preferred_element_type=F32)
        slc = jnp.dot(score, msel_ref[...], precision=HIGHEST, preferred_element_type=F32)
        r = lax.broadcasted_iota(jnp.int32, slc.shape, 0)
        tpos = PAST_LEN + r % DEC_SEQ
        blk = lax.broadcasted_iota(jnp.int32, slc.shape, 1)
        sel = _select_mask(slc, tpos, blk, S_NSEL, N_SELECT).astype(BF16)
        sel_scr[...] = _tn_dot(sel, gexp_ref[...])
        pw = _softmax_cols(jnp.dot(kw_ref[0], qt, preferred_element_type=F32) + bw_ref[...])
        o_ref[0, 2] = _tn_dot(vw_ref[0], pw.astype(BF16))
        m_scr[...] = jnp.full(m_scr.shape, NEG_INF, F32)
        l_scr[...] = jnp.zeros(l_scr.shape, F32)
        acc_scr[...] = jnp.zeros(acc_scr.shape, F32)

    for cp in copies(b, c, slot):
        cp.wait()

    def update(st, v):
        m_old = m_scr[...]
        m_new = jnp.maximum(m_old, jnp.max(st, axis=0, keepdims=True))
        a = jnp.exp(m_old - m_new)
        et = jnp.where(st > 0.5 * NEG_INF, jnp.exp(st - m_new), 0.0)
        l_scr[...] = a * l_scr[...] + jnp.sum(et, axis=0, keepdims=True)
        acc_scr[...] = a * acc_scr[...] + _tn_dot(v, et.astype(BF16))
        m_scr[...] = m_new

    nk = SEL_PAGES * PAGE_SIZE
    k = buf[slot, :, 0:kw].astype(BF16)
    v = buf[slot, :, kw:2 * kw].astype(BF16)
    st = jnp.dot(k, qt, preferred_element_type=F32)
    far = jnp.broadcast_to(cf_ref[...], (PAGE_SIZE, S_COLS))
    last = jnp.where(c == nch - 1, bl_ref[...], far)
    st = jnp.concatenate([st[:nk - PAGE_SIZE] + cf_ref[...], st[nk - PAGE_SIZE:] + last], axis=0)
    blocks = nk // SEL_BLOCK
    selc = sel_scr[pl.ds(pl.multiple_of(c * blocks, blocks), blocks), :].astype(BF16)
    keymask = jnp.dot(eexp_ref[...], selc, preferred_element_type=F32)
    update(jnp.where(keymask > 0.5, st, NEG_INF), v)

    @pl.when(c == nch - 1)
    def _():
        sn = jnp.dot(kn_ref[0], qt, preferred_element_type=F32) + bn_ref[...]
        new_blk = PAST_LEN // SEL_BLOCK
        sn = jnp.where(sel_scr[new_blk:new_blk + 1, :] > 0.5, sn, NEG_INF)
        update(sn, vn_ref[0])
        o_ref[0, 1] = acc_scr[...] / jnp.maximum(l_scr[...], TINY)


def _nsa_sample(page_table, qt, kc, vc, kwin, vwin, knew, vnew, b_cmp, b_win, b_last, b_new, c_far, pages):
    nch = N_PAGES // SEL_PAGES
    n_cmp = PAST_LEN // CMP_STRIDE
    kwid = A_KV_HEADS * HEAD_DIM
    gsum = np.zeros((A_KV_HEADS * DEC_SEQ, S_COLS), np.float32)
    for k in range(A_KV_HEADS):
        for g in range(A_GROUP):
            for t in range(DEC_SEQ):
                gsum[k * DEC_SEQ + t, (k * A_GROUP + g) * DEC_SEQ + t] = 1.0
    msel = _overlap_matrix(n_cmp - 1, S_NSEL, n_cmp, S_NSEL_PAD)
    nk = SEL_PAGES * PAGE_SIZE
    eexp = np.zeros((nk, nk // SEL_BLOCK), np.float32)
    eexp[np.arange(nk), np.arange(nk) // SEL_BLOCK] = 1.0
    per_b = lambda shape: pl.BlockSpec((1,) + shape, lambda b, c, pid: (b, 0, 0))
    const2 = lambda shape: pl.BlockSpec(shape, lambda b, c, pid: (0, 0))
    nwin = kwin.shape[1]
    grid_spec = pltpu.PrefetchScalarGridSpec(
        num_scalar_prefetch=1,
        grid=(DEC_BATCH, nch),
        in_specs=[per_b((kwid, S_COLS)),
                  per_b((n_cmp, kwid)), per_b((n_cmp, kwid)),
                  per_b((nwin, kwid)), per_b((nwin, kwid)),
                  per_b((16, kwid)), per_b((16, kwid)),
                  const2((n_cmp, S_COLS)), const2((nwin, S_COLS)), const2((PAGE_SIZE, S_COLS)), const2((16, S_COLS)),
                  const2((1, S_COLS)),
                  const2(gsum.shape), const2(msel.shape), const2(gsum.shape), const2(eexp.shape),
                  pl.BlockSpec(memory_space=pl.ANY)],
        out_specs=pl.BlockSpec((1, 3, kwid, S_COLS), lambda b, c, pid: (b, 0, 0, 0)),
        scratch_shapes=[pltpu.VMEM((2, nk, 2 * kwid), F32),
                        pltpu.SemaphoreType.DMA((2,)),
                        pltpu.VMEM((S_NSEL_PAD, S_COLS), F32),
                        pltpu.VMEM((1, S_COLS), F32), pltpu.VMEM((1, S_COLS), F32),
                        pltpu.VMEM((kwid, S_COLS), F32)],
    )
    return pl.pallas_call(
        functools.partial(_nsa_sample_body, npg=N_PAGES),
        grid_spec=grid_spec,
        out_shape=jax.ShapeDtypeStruct((DEC_BATCH, 3, kwid, S_COLS), F32),
        compiler_params=_params(("arbitrary", "arbitrary")),
        name="nsa_sample",
    )(page_table, qt, kc, vc, kwin, vwin, knew, vnew, b_cmp, b_win, b_last, b_new, c_far,
      jnp.asarray(gsum), jnp.asarray(msel), jnp.asarray(gsum, BF16), jnp.asarray(eexp, BF16), pages)


def _combine_body(oc_ref, os_ref, ow_ref, gl_ref, ex_ref, o_ref):
    gates = 1.0 / (1.0 + jnp.exp(-gl_ref[...]))
    acc = None
    for br, r in enumerate((oc_ref, os_ref, ow_ref)):
        g = jnp.dot(gates[:, br * N_HEADS:(br + 1) * N_HEADS], ex_ref[...], precision=HIGHEST,
                    preferred_element_type=F32)
        term = g * r[...]
        acc = term if acc is None else acc + term
    o_ref[...] = acc.astype(o_ref.dtype)


def _combine(oc, os_, ow, gate_logits, *, tm):
    m = oc.shape[0]
    expand = np.zeros((N_HEADS, Q_WIDTH), np.float32)
    expand[np.arange(Q_WIDTH) // HEAD_DIM, np.arange(Q_WIDTH)] = 1.0
    row = lambda w: pl.BlockSpec((tm, w), lambda i: (i, 0))
    return pl.pallas_call(
        _combine_body,
        grid=(m // tm,),
        in_specs=[row(Q_WIDTH), row(Q_WIDTH), row(Q_WIDTH), row(3 * N_HEADS),
                  pl.BlockSpec((N_HEADS, Q_WIDTH), lambda i: (0, 0))],
        out_specs=row(Q_WIDTH),
        out_shape=jax.ShapeDtypeStruct((m, Q_WIDTH), BF16),
        compiler_params=_params(("arbitrary",)),
        name="nsa_combine",
    )(oc, os_, ow, gate_logits, jnp.asarray(expand))


def _swa_prompt_body(q_ref, k_ref, v_ref, bias_ref, sink_ref, o_ref):
    i = pl.program_id(2)
    grp = B_GROUP
    rows = grp * QT
    qblk = q_ref[...]
    q = jnp.concatenate([qblk[:, g * HEAD_DIM:(g + 1) * HEAD_DIM] for g in range(grp)], axis=0)
    bias = bias_ref[0].reshape(rows, 2 * QT)
    parts = []
    starts = []
    for u in range(2):
        j = i - 1 + u
        start = pl.multiple_of(jnp.maximum(j, 0) * QT, QT)
        starts.append(start)
        su = _nt_dot(q, k_ref[0, 0, pl.ds(start, QT), :]) + bias[:, u * QT:(u + 1) * QT]
        parts.append(jnp.where(j >= 0, su, NEG_INF))
    s = jnp.concatenate(parts, axis=1)
    sink = sink_ref[0]
    mx = jnp.maximum(jnp.max(s, axis=1, keepdims=True), sink)
    e = jnp.where(s > 0.5 * NEG_INF, jnp.exp(s - mx), 0.0)
    den = jnp.sum(e, axis=1, keepdims=True) + jnp.exp(sink - mx)
    p = (e / jnp.maximum(den, TINY)).astype(BF16)
    o = jnp.zeros((rows, HEAD_DIM), F32)
    for u in range(2):
        o = o + jnp.dot(p[:, u * QT:(u + 1) * QT], v_ref[0, 0, pl.ds(starts[u], QT), :], preferred_element_type=F32)
    o_ref[...] = jnp.concatenate([o[g * QT:(g + 1) * QT] for g in range(grp)], axis=1).astype(o_ref.dtype)


def _swa_prompt(q, k, v, bias, sink_rows):
    nq = SEQ // QT
    grp = B_GROUP
    kv_spec = pl.BlockSpec((1, 1, SEQ, HEAD_DIM), lambda b, h, i: (b, h, 0, 0))
    blk = pl.BlockSpec((QT, grp * HEAD_DIM), lambda b, h, i: (b * nq + i, h))
    return pl.pallas_call(
        _swa_prompt_body,
        grid=(BATCH, B_KV_HEADS, nq),
        in_specs=[blk, kv_spec, kv_spec,
                  pl.BlockSpec((1, grp, QT, 2 * QT), lambda b, h, i: (h, 0, 0, 0)),
                  pl.BlockSpec((1, grp * QT, 1), lambda b, h, i: (h, 0, 0))],
        out_specs=blk,
        out_shape=jax.ShapeDtypeStruct((MP, Q_WIDTH), BF16),
        compiler_params=_params(("arbitrary", "arbitrary", "arbitrary")),
        name="swa_prompt",
    )(q, k, v, bias, sink_rows)


def _swa_sample_body(qt_ref, k_ref, v_ref, bias_ref, sink_ref, o_ref):
    s = jnp.dot(k_ref[0], qt_ref[0], preferred_element_type=F32) + bias_ref[...]
    sink = sink_ref[...]
    mx = jnp.maximum(jnp.max(s, axis=0, keepdims=True), sink)
    e = jnp.where(s > 0.5 * NEG_INF, jnp.exp(s - mx), 0.0)
    den = jnp.sum(e, axis=0, keepdims=True) + jnp.exp(sink - mx)
    p = e / jnp.maximum(den, TINY)
    o_ref[0] = _tn_dot(v_ref[0], p.astype(BF16))


def _swa_sample(qt, k, v, bias_t, sink_row):
    nk = k.shape[1]
    w = B_KV_HEADS * HEAD_DIM
    per_b = lambda shape: pl.BlockSpec((1,) + shape, lambda b: (b, 0, 0))
    return pl.pallas_call(
        _swa_sample_body,
        grid=(DEC_BATCH,),
        in_specs=[per_b((w, S_COLS)), per_b((nk, w)), per_b((nk, w)),
                  pl.BlockSpec((nk, S_COLS), lambda b: (0, 0)), pl.BlockSpec((1, S_COLS), lambda b: (0, 0))],
        out_specs=per_b((w, S_COLS)),
        out_shape=jax.ShapeDtypeStruct((DEC_BATCH, w, S_COLS), F32),
        compiler_params=_params(("arbitrary",)),
        name="swa_sample",
    )(qt, k, v, bias_t, sink_row)


MOE_TM = 512
MOE_TILES = -(-(TOP_K * MT + N_EXPERTS * (MOE_TM - 1)) // MOE_TM)
MOE_ROWS = MOE_TILES * MOE_TM
MOE_TN_UP = 512
MOE_TN_DOWN = 512
MOE_SUB = 256
MOE_KSPLIT = 2


def _router_body(x_ref, w_ref, b_ref, o_ref):
    logits = jnp.dot(x_ref[...], w_ref[...], precision=HIGHEST, preferred_element_type=F32) + b_ref[...]
    lane = lax.broadcasted_iota(jnp.int32, logits.shape, 1)
    logits = jnp.where(lane < N_EXPERTS, logits, -jnp.inf)
    v1 = jnp.max(logits, axis=1, keepdims=True)
    i1 = jnp.min(jnp.where(logits == v1, lane, LANES), axis=1, keepdims=True)
    rest = jnp.where(lane == i1, -jnp.inf, logits)
    v2 = jnp.max(rest, axis=1, keepdims=True)
    i2 = jnp.min(jnp.where(rest == v2, lane, LANES), axis=1, keepdims=True)
    e2 = jnp.exp(v2 - v1)
    den = 1.0 + e2
    out = jnp.where(lane == 0, i1.astype(F32),
                    jnp.where(lane == 1, i2.astype(F32),
                              jnp.where(lane == 2, 1.0 / den, jnp.where(lane == 3, e2 / den, 0.0))))
    o_ref[...] = out


def _router(x, w, b, *, tm):
    m, d = x.shape
    wp = jnp.zeros((d, LANES), F32).at[:, :N_EXPERTS].set(w)
    bp = jnp.zeros((1, LANES), F32).at[0, :N_EXPERTS].set(b)
    return pl.pallas_call(
        _router_body,
        grid=(m // tm,),
        in_specs=[pl.BlockSpec((tm, d), lambda i: (i, 0)), pl.BlockSpec((d, LANES), lambda i: (0, 0)),
                  pl.BlockSpec((1, LANES), lambda i: (0, 0))],
        out_specs=pl.BlockSpec((tm, LANES), lambda i: (i, 0)),
        out_shape=jax.ShapeDtypeStruct((m, LANES), F32),
        compiler_params=_params(("arbitrary",)),
        name="moe_router",
    )(x, wp, bp)


def _gather_body(idx_ref, src_ref, o_ref, sem, *, rows):
    base = pl.program_id(0) * rows

    def row_copy(r, src_row):
        return pltpu.make_async_copy(src_ref.at[pl.ds(src_row, 1)], o_ref.at[pl.ds(r, 1)], sem)

    def issue(r, carry):
        row_copy(r, idx_ref[base + r]).start()
        return carry

    def drain(r, carry):
        row_copy(r, 0).wait()
        return carry

    lax.fori_loop(0, rows, issue, 0)
    lax.fori_loop(0, rows, drain, 0)


def _gather_rows(src, idx, *, rows, name):
    n = idx.shape[0]
    w = src.shape[1]
    grid_spec = pltpu.PrefetchScalarGridSpec(
        num_scalar_prefetch=1,
        grid=(n // rows,),
        in_specs=[pl.BlockSpec(memory_space=pl.ANY)],
        out_specs=pl.BlockSpec((rows, w), lambda i, idx_ref: (i, 0)),
        scratch_shapes=[pltpu.SemaphoreType.DMA(())],
    )
    return pl.pallas_call(
        functools.partial(_gather_body, rows=rows),
        grid_spec=grid_spec,
        out_shape=jax.ShapeDtypeStruct((n, w), src.dtype),
        compiler_params=_params(("arbitrary",)),
        name=name,
    )(idx, src)


def _moe_up_body(te_ref, nv_ref, x_ref, wg_ref, wu_ref, o_ref, wg_b, wu_b):
    i = pl.program_id(1)
    e = te_ref[i]
    prev = te_ref[jnp.maximum(i - 1, 0)]

    @pl.when((i == 0) | (e != prev))
    def _():
        wg_b[...] = wg_ref[0].astype(BF16)
        wu_b[...] = wu_ref[0].astype(BF16)

    @pl.when(i < nv_ref[0])
    def _():
        x = x_ref[...]
        g = jnp.dot(x, wg_b[...], preferred_element_type=F32)
        u = jnp.dot(x, wu_b[...], preferred_element_type=F32)
        o_ref[...] = (_silu(g) * u).astype(o_ref.dtype)

    @pl.when(i >= nv_ref[0])
    def _():
        o_ref[...] = jnp.zeros(o_ref.shape, o_ref.dtype)


def _moe_up(tile_expert, n_valid, xs, w_in):
    d = xs.shape[1]
    f = w_in.shape[2] // 2
    nb = f // MOE_TN_UP
    grid_spec = pltpu.PrefetchScalarGridSpec(
        num_scalar_prefetch=2,
        grid=(nb, MOE_TILES),
        in_specs=[pl.BlockSpec((MOE_TM, d), lambda j, i, te, nv: (i, 0)),
                  pl.BlockSpec((1, d, MOE_TN_UP), lambda j, i, te, nv: (te[i], 0, j)),
                  pl.BlockSpec((1, d, MOE_TN_UP), lambda j, i, te, nv: (te[i], 0, j + nb))],
        out_specs=pl.BlockSpec((MOE_TM, MOE_TN_UP), lambda j, i, te, nv: (i, j)),
        scratch_shapes=[pltpu.VMEM((d, MOE_TN_UP), BF16), pltpu.VMEM((d, MOE_TN_UP), BF16)],
    )
    return pl.pallas_call(
        _moe_up_body,
        grid_spec=grid_spec,
        out_shape=jax.ShapeDtypeStruct((MOE_ROWS, f), BF16),
        compiler_params=_params(("arbitrary", "arbitrary")),
        name="moe_up",
    )(tile_expert, n_valid, xs, w_in, w_in)


def _moe_down_body(te_ref, nv_ref, *refs, first):
    if first:
        x_ref, w_ref, o_ref, w_b = refs
        prev_ref = None
    else:
        x_ref, w_ref, prev_ref, o_ref, w_b = refs
    i = pl.program_id(1)
    per = MOE_TM // MOE_SUB
    e = te_ref[i // per]
    prev = te_ref[jnp.maximum(i - 1, 0) // per]

    @pl.when((i == 0) | (e != prev))
    def _():
        w_b[...] = w_ref[0].astype(BF16)

    @pl.when(i // per < nv_ref[0])
    def _():
        acc = jnp.dot(x_ref[...], w_b[...], preferred_element_type=F32)
        o_ref[...] = acc if first else acc + prev_ref[...]

    @pl.when(i // per >= nv_ref[0])
    def _():
        o_ref[...] = jnp.zeros(o_ref.shape, o_ref.dtype)


def _moe_down(tile_expert, n_valid, h, w_out, prev, *, part):
    f = h.shape[1]
    d = w_out.shape[2]
    tk = f // MOE_KSPLIT
    per = MOE_TM // MOE_SUB
    first = prev is None
    in_specs = [pl.BlockSpec((MOE_SUB, tk), lambda j, i, te, nv: (i, part)),
                pl.BlockSpec((1, tk, MOE_TN_DOWN), lambda j, i, te, nv: (te[i // per], part, j))]
    args = [h, w_out]
    aliases = {}
    if not first:
        in_specs.append(pl.BlockSpec((MOE_SUB, MOE_TN_DOWN), lambda j, i, te, nv: (i, j)))
        args.append(prev)
        aliases = {4: 0}
    grid_spec = pltpu.PrefetchScalarGridSpec(
        num_scalar_prefetch=2,
        grid=(d // MOE_TN_DOWN, MOE_TILES * per),
        in_specs=in_specs,
        out_specs=pl.BlockSpec((MOE_SUB, MOE_TN_DOWN), lambda j, i, te, nv: (i, j)),
        scratch_shapes=[pltpu.VMEM((tk, MOE_TN_DOWN), BF16)],
    )
    return pl.pallas_call(
        functools.partial(_moe_down_body, first=first),
        grid_spec=grid_spec,
        out_shape=jax.ShapeDtypeStruct((MOE_ROWS, d), F32),
        input_output_aliases=aliases,
        compiler_params=_params(("arbitrary", "arbitrary")),
        name=f"moe_down_{part}",
    )(tile_expert, n_valid, *args)


def _route(top):
    idx = top[:, 0:TOP_K].astype(jnp.int32)
    gate = top[:, TOP_K:2 * TOP_K]
    flat_e = idx.reshape(-1)
    onehot = (flat_e[:, None] == jnp.arange(N_EXPERTS)[None, :]).astype(jnp.int32)
    csum = jnp.cumsum(onehot, axis=0)
    counts = csum[-1]
    rank = jnp.take_along_axis(csum, flat_e[:, None], axis=1)[:, 0] - 1
    tiles_per = (counts + MOE_TM - 1) // MOE_TM
    tile_end = jnp.cumsum(tiles_per)
    tile_start = tile_end - tiles_per
    pos = tile_start[flat_e] * MOE_TM + rank
    src_tok = jnp.zeros((MOE_ROWS,), jnp.int32).at[pos].set(jnp.arange(TOP_K * MT, dtype=jnp.int32) // TOP_K)
    n_valid = tile_end[-1]
    tile_ids = jnp.arange(MOE_TILES, dtype=jnp.int32)
    tile_expert = jnp.searchsorted(tile_end, jnp.minimum(tile_ids, n_valid - 1), side="right").astype(jnp.int32)
    tile_expert = jnp.minimum(tile_expert, N_EXPERTS - 1)
    return gate, pos.reshape(MT, TOP_K), src_tok, tile_expert, n_valid.reshape(1).astype(jnp.int32)


def _moe(x_f32, x_bf16, w_router, b_router, w_in, w_out):
    top = _router(x_f32, w_router, b_router, tm=264)
    gate, pos, src_tok, tile_expert, n_valid = _route(top)
    x_words = lax.bitcast_convert_type(x_bf16.reshape(MT, D_MODEL // 2, 2), jnp.uint32)
    xs_words = _gather_rows(x_words, src_tok, rows=MOE_SUB, name="moe_dispatch")
    xs = lax.bitcast_convert_type(xs_words, BF16).reshape(MOE_ROWS, D_MODEL)
    h = _moe_up(tile_expert, n_valid, xs, w_in)
    y = None
    for part in range(MOE_KSPLIT):
        y = _moe_down(tile_expert, n_valid, h, w_out, y, part=part)
    back = jnp.concatenate([pos[:, 0], pos[:, 1]])
    yy = _gather_rows(y, back, rows=MOE_SUB, name="moe_collect")
    return yy, gate


def _heads_first(x, b, t, heads):
    return x.reshape(b, t, heads, HEAD_DIM).transpose(0, 2, 1, 3)


def _block_q(q, kvh, grp):
    q5 = q.reshape(DEC_BATCH, DEC_SEQ, kvh, grp, HEAD_DIM)
    eye = jnp.eye(kvh, dtype=q.dtype)
    qt = jnp.einsum("btkgd,kl->bldkgt", q5, eye)
    return qt.reshape(DEC_BATCH, kvh * HEAD_DIM, kvh * grp * DEC_SEQ)


def _unblock_o(ot, kvh, grp):
    o6 = ot.reshape(DEC_BATCH, kvh, HEAD_DIM, kvh, grp, DEC_SEQ)
    o = jnp.einsum("bkdkgt->btkgd", o6)
    return o.reshape(MS, kvh * grp * HEAD_DIM)


def _cols_t(tab):
    return tab.transpose(2, 0, 1).reshape(tab.shape[2], N_HEADS * DEC_SEQ)


def kernel(x_prompt, x_sample, cache_cmp_kv, cache_sel_kv, state_win_kv, state_shared_kv, page_table, rel_table, ln_g, ln_b, a_w_in, a_cmp_pe, a_cmp_w1, a_cmp_b1, a_cmp_w2, a_w_out, b_w_kv, b_w_q, b_sinks, b_w_out, dense_w_in, dense_w_out, moe_router_w, moe_router_b, moe_w_in, moe_w_out):
    n_phys = cache_cmp_kv.shape[1]
    x0 = jnp.concatenate([x_prompt.reshape(MP, D_MODEL), x_sample.reshape(MS, D_MODEL)], axis=0)
    x0b = x0.astype(BF16)

    dist = _dist_tables()
    tabs = {k: _bias_table(rel_table, d, lo, hi, "bias_" + k) for k, (d, lo, hi) in dist.items()}
    c_far = jnp.repeat(rel_table[REL_BUCKETS - 1], DEC_SEQ).reshape(1, N_HEADS * DEC_SEQ)

    la = 0
    w_in = a_w_in[la]
    kvg_w = jnp.pad(w_in[:, Q_WIDTH:], ((0, 0), (0, 64))).astype(BF16)
    q = _matmul(x0b, w_in[:, :Q_WIDTH].astype(BF16), tm=TM, tn=1024, out_dtype=BF16, scale=SCALE, name="a_q_proj")
    hk = _matmul(x0b, kvg_w, tm=TM, tn=896, out_dtype=F32, name="a_kv_proj")
    kvw = A_KV_WIDTH
    kv_c, kv_s, kv_w = hk[:, 0:kvw], hk[:, kvw:2 * kvw], hk[:, 2 * kvw:3 * kvw]
    gate_logits = hk[:, 3 * kvw:3 * kvw + 3 * N_HEADS]

    w1 = a_cmp_w1[la]
    eye2 = jnp.eye(2, dtype=F32)
    wpair = jnp.einsum("cqpdh,kl->cqpkdlh", w1.reshape(2, CMP_PAIRS, 2, HEAD_DIM, CMP_HIDDEN), eye2)
    wpair = wpair.reshape(2, CMP_PAIRS, 4 * HEAD_DIM, 2 * CMP_HIDDEN).astype(BF16)
    w2bd = jnp.einsum("chd,kl->ckhld", a_cmp_w2[la], eye2).reshape(2, 2 * CMP_HIDDEN, 2 * HEAD_DIM).astype(BF16)
    pe8 = jnp.broadcast_to(a_cmp_pe[la].reshape(2, 1, CMP_BLOCK * HEAD_DIM), (2, 8, CMP_BLOCK * HEAD_DIM)).astype(BF16)
    w1f = w1.reshape(2, CMP_BLOCK * HEAD_DIM, CMP_HIDDEN).astype(BF16)
    b1 = a_cmp_b1[la].reshape(2, 1, CMP_HIDDEN)

    pages_per_seq = SEQ // PAGE_SIZE
    prompt_ids = jnp.arange(BATCH * pages_per_seq, dtype=jnp.int32).reshape(BATCH, pages_per_seq)
    ckv_p = _compress(prompt_ids, kv_c[:MP].reshape(BATCH * pages_per_seq, PAGE_SIZE, kvw), wpair, w2bd, pe8, w1f, b1,
                      pc=pages_per_seq, name="cmp_prompt")
    sample_ids = la * n_phys + page_table
    ckv_s = _compress(sample_ids, cache_cmp_kv.reshape(-1, PAGE_SIZE, kvw), wpair, w2bd, pe8, w1f, b1,
                      pc=32, name="cmp_sample")

    def cmp_heads(ckv, n):
        bsz = ckv.shape[0]
        return ckv.reshape(bsz, 2, 2, n, 2, HEAD_DIM).transpose(0, 1, 2, 4, 3, 5).reshape(bsz, 2, A_KV_HEADS, n, HEAD_DIM)

    ckv_ph = cmp_heads(ckv_p, SEQ // CMP_STRIDE)

    def kv_heads(kv):
        kv5 = kv.reshape(BATCH, SEQ, 2, A_KV_HEADS, HEAD_DIM).astype(BF16).transpose(2, 0, 3, 1, 4)
        return kv5[0], kv5[1]

    ks_p, vs_p = kv_heads(kv_s[:MP])
    kw_p, vw_p = kv_heads(kv_w[:MP])
    grp = A_GROUP
    bias_cmp = tabs["p_cmp"].reshape(A_KV_HEADS, grp, SEQ, SEQ // CMP_STRIDE)
    bias_win = tabs["p_win"].reshape(A_KV_HEADS, grp, QT, A_WINDOW + QT)
    bias_sel = tabs["p_sel"].reshape(A_KV_HEADS, grp, 3, QT, QT).transpose(0, 2, 1, 3, 4).reshape(A_KV_HEADS, 3, grp * QT, QT)
    oc_p, os_p, ow_p = _nsa_prompt(q, ckv_ph[:, 0], ckv_ph[:, 1], ks_p, vs_p, kw_p, vw_p, bias_cmp, bias_win, bias_sel)

    q_s = q[MP:].reshape(DEC_BATCH, DEC_SEQ, Q_WIDTH)
    qt_a = _block_q(q_s, A_KV_HEADS, A_GROUP)
    n_cmp_s = PAST_LEN // CMP_STRIDE
    ckv_sa = ckv_s.reshape(DEC_BATCH, 2, 2, n_cmp_s, 2 * HEAD_DIM).transpose(0, 1, 3, 2, 4).reshape(DEC_BATCH, 2, n_cmp_s, 4 * HEAD_DIM)
    kv_w_new = kv_w[MP:].reshape(DEC_BATCH, DEC_SEQ, kvw)
    win_full = jnp.concatenate([state_win_kv[la].reshape(DEC_BATCH, A_WINDOW, kvw), kv_w_new], axis=1)
    win_pad = jnp.pad(win_full, ((0, 0), (0, 8), (0, 0))).astype(BF16)
    kv_s_new = kv_s[MP:].reshape(DEC_BATCH, DEC_SEQ, kvw)
    new_pad = jnp.pad(kv_s_new, ((0, 0), (0, 16 - DEC_SEQ), (0, 0))).astype(BF16)
    half = kvw // 2
    ot_a = _nsa_sample(sample_ids, qt_a, ckv_sa[:, 0], ckv_sa[:, 1], win_pad[..., :half], win_pad[..., half:],
                       new_pad[..., :half], new_pad[..., half:],
                       _cols_t(tabs["s_cmp"]), _cols_t(tabs["s_win"]), _cols_t(tabs["s_last"]), _cols_t(tabs["s_new"]),
                       c_far, cache_sel_kv.reshape(-1, PAGE_SIZE, kvw))
    o_s3 = [_unblock_o(ot_a[:, br], A_KV_HEADS, A_GROUP) for br in range(3)]

    oc = jnp.concatenate([oc_p, o_s3[0]], axis=0)
    os_ = jnp.concatenate([os_p, o_s3[1]], axis=0)
    ow = jnp.concatenate([ow_p, o_s3[2]], axis=0)
    o_a = _combine(oc, os_, ow, gate_logits, tm=264)
    y = _matmul(o_a, a_w_out[la].astype(BF16), tm=TM, tn=1024, out_dtype=F32, name="a_out_proj")
    row_map = lambda i: (i, 0)
    x1, x1b = _residual_ln(x0, [y], [row_map], None, ln_g[0, 0], ln_b[0, 0], tm=192, name="ln_0a")

    hmid = _matmul_swiglu(x1b, dense_w_in[0].astype(BF16), tm=TM, tn=256, name="dense_up")
    f = _matmul(hmid, dense_w_out[0].astype(BF16), tm=528, tn=512, out_dtype=F32, name="dense_down")
    x2, x2b = _residual_ln(x1, [f], [row_map], None, ln_g[0, 1], ln_b[0, 1], tm=192, name="ln_0b")

    sh = _matmul(x2b, b_w_kv.astype(BF16), tm=TM, tn=1024, out_dtype=F32, name="b_kv_proj")
    q1 = _matmul(x2b, b_w_q[0].astype(BF16), tm=TM, tn=1024, out_dtype=BF16, scale=SCALE, name="b_q_proj")
    shw = 2 * B_KV_HEADS * HEAD_DIM
    sh_p = sh[:MP].reshape(BATCH, SEQ, 2, B_KV_HEADS, HEAD_DIM)
    sh_new = sh[MP:].reshape(DEC_BATCH, DEC_SEQ, shw)
    sh5 = sh_p.astype(BF16).transpose(2, 0, 3, 1, 4)
    bias_swa = tabs["p_swa"].reshape(B_KV_HEADS, B_GROUP, QT, 2 * QT)
    sink_rows = jnp.broadcast_to(b_sinks[0].reshape(B_KV_HEADS, B_GROUP, 1, 1), (B_KV_HEADS, B_GROUP, QT, 1)).reshape(B_KV_HEADS, B_GROUP * QT, 1)
    o_bp = _swa_prompt(q1, sh5[0], sh5[1], bias_swa, sink_rows)

    sh_full = jnp.concatenate([state_shared_kv.reshape(DEC_BATCH, B_WINDOW, shw), sh_new], axis=1)
    sh_pad = jnp.pad(sh_full, ((0, 0), (0, 8), (0, 0))).astype(BF16)
    qt_b = _block_q(q1[MP:].reshape(DEC_BATCH, DEC_SEQ, Q_WIDTH), B_KV_HEADS, B_GROUP)
    sink_row = jnp.repeat(b_sinks[0], DEC_SEQ).reshape(1, N_HEADS * DEC_SEQ)
    ot_b = _swa_sample(qt_b, sh_pad[..., :shw // 2], sh_pad[..., shw // 2:], _cols_t(tabs["s_swa"]), sink_row)
    o_bs = _unblock_o(ot_b, B_KV_HEADS, B_GROUP).astype(BF16)
    o_b = jnp.concatenate([o_bp, o_bs], axis=0)
    y = _matmul(o_b, b_w_out[0].astype(BF16), tm=TM, tn=1024, out_dtype=F32, name="b_out_proj")
    x3, x3b = _residual_ln(x2, [y], [row_map], None, ln_g[1, 0], ln_b[1, 0], tm=192, name="ln_1a")

    yy, gate = _moe(x3, x3b, moe_router_w[0], moe_router_b[0], moe_w_in[0], moe_w_out[0])
    nblk = MT // 192
    x4, _ = _residual_ln(x3, [yy, yy], [row_map, lambda i: (i + nblk, 0)], gate, ln_g[1, 1], ln_b[1, 1], tm=192, name="ln_1b")

    y_prompt = x4[:MP].reshape(BATCH, SEQ, D_MODEL)
    y_sample = x4[MP:].reshape(DEC_BATCH, DEC_SEQ, D_MODEL)
    kv6 = lambda a, bsz, t: a.reshape(1, bsz, t, 2, A_KV_HEADS, HEAD_DIM)
    new_cmp_p = kv6(kv_c[:MP], BATCH, SEQ)
    new_cmp_s = kv6(kv_c[MP:], DEC_BATCH, DEC_SEQ)
    new_sel_p = kv6(kv_s[:MP], BATCH, SEQ)
    new_sel_s = kv6(kv_s[MP:], DEC_BATCH, DEC_SEQ)
    new_win_p = kv6(kv_w[:MP], BATCH, SEQ)[:, :, SEQ - A_WINDOW:]
    new_win_s = kv6(win_full[:, DEC_SEQ:], DEC_BATCH, A_WINDOW)
    new_sh_p = sh_p[:, SEQ - B_WINDOW:]
    new_sh_s = sh_full[:, DEC_SEQ:].reshape(DEC_BATCH, B_WINDOW, 2, B_KV_HEADS, HEAD_DIM)
    return (y_prompt, y_sample, new_cmp_p, new_cmp_s, new_sel_p, new_sel_s, new_win_p, new_win_s, new_sh_p, new_sh_s)
```

```python
import functools
import math

import jax
import jax.numpy as jnp
import numpy as np
from jax import lax
from jax.experimental import pallas as pl
from jax.experimental.pallas import tpu as pltpu

D_MODEL = 4096
BATCH = 4
SEQ = 2048
DEPTH = 2
DEC_BATCH = 32
DEC_SEQ = 8
PAST_LEN = 16384
PAGE_SIZE = 128
N_HEADS = 64
HEAD_DIM = 64
Q_WIDTH = N_HEADS * HEAD_DIM
SCALE = HEAD_DIM ** -0.5
A_KV_HEADS = 4
A_GROUP = N_HEADS // A_KV_HEADS
A_KV_WIDTH = 2 * A_KV_HEADS * HEAD_DIM
CMP_BLOCK = 32
CMP_STRIDE = 16
CMP_HIDDEN = 256
SEL_BLOCK = 64
N_SELECT = 16
N_LOCAL = 2
A_WINDOW = 512
B_KV_HEADS = 8
B_GROUP = N_HEADS // B_KV_HEADS
B_WINDOW = 128
REL_BUCKETS = 32
REL_MAX_DIST = 128
D_FF = 11008
N_EXPERTS = 8
TOP_K = 2
D_FF_EXPERT = 14336
ALPHA = (2 * DEPTH) ** 0.25
LN_EPS = 1e-5
NEG_INF = -1e30
TINY = 1e-30
FORCE = 1e9

LANES = 128
MP = BATCH * SEQ
MS = DEC_BATCH * DEC_SEQ
MT = MP + MS
TM = 1056
QT = 128
N_PAGES = PAST_LEN // PAGE_SIZE
VMEM_LIMIT = 56 * 1024 * 1024

F32 = jnp.float32
BF16 = jnp.bfloat16
HIGHEST = lax.Precision.HIGHEST


def _params(sem, vmem=VMEM_LIMIT):
    return pltpu.CompilerParams(dimension_semantics=sem, vmem_limit_bytes=vmem)


def _silu(x):
    return x * (1.0 / (1.0 + jnp.exp(-x)))


def _nt_dot(a, b):
    return lax.dot_general(a, b, (((1,), (1,)), ((), ())), preferred_element_type=F32)


def _tn_dot(a, b):
    return lax.dot_general(a, b, (((0,), (0,)), ((), ())), preferred_element_type=F32)


def _mm_body(x_ref, w_ref, o_ref, *, scale):
    acc = jnp.dot(x_ref[...], w_ref[...], preferred_element_type=F32)
    if scale != 1.0:
        acc = acc * scale
    o_ref[...] = acc.astype(o_ref.dtype)


def _matmul(x, w, *, tm, tn, out_dtype, scale=1.0, name):
    m, k = x.shape
    n = w.shape[1]
    return pl.pallas_call(
        functools.partial(_mm_body, scale=scale),
        grid=(n // tn, m // tm),
        in_specs=[pl.BlockSpec((tm, k), lambda j, i: (i, 0)),
                  pl.BlockSpec((k, tn), lambda j, i: (0, j))],
        out_specs=pl.BlockSpec((tm, tn), lambda j, i: (i, j)),
        out_shape=jax.ShapeDtypeStruct((m, n), out_dtype),
        compiler_params=_params(("arbitrary", "arbitrary")),
        name=name,
    )(x, w)


def _mm_swiglu_body(x_ref, wg_ref, wu_ref, o_ref):
    x = x_ref[...]
    g = jnp.dot(x, wg_ref[...], preferred_element_type=F32)
    u = jnp.dot(x, wu_ref[...], preferred_element_type=F32)
    o_ref[...] = (_silu(g) * u).astype(o_ref.dtype)


def _matmul_swiglu(x, w_in, *, tm, tn, name):
    m, k = x.shape
    f = w_in.shape[1] // 2
    nb = f // tn
    return pl.pallas_call(
        _mm_swiglu_body,
        grid=(m // tm, nb),
        in_specs=[pl.BlockSpec((tm, k), lambda i, j: (i, 0)),
                  pl.BlockSpec((k, tn), lambda i, j: (0, j)),
                  pl.BlockSpec((k, tn), lambda i, j: (0, j + nb))],
        out_specs=pl.BlockSpec((tm, tn), lambda i, j: (i, j)),
        out_shape=jax.ShapeDtypeStruct((m, f), BF16),
        compiler_params=_params(("arbitrary", "arbitrary")),
        name=name,
    )(x, w_in, w_in)


def _ln_body(*refs, alpha, n_add, gated):
    x_ref = refs[0]
    add_refs = refs[1:1 + n_add]
    pos = 1 + n_add
    gate_ref = refs[pos] if gated else None
    pos += 1 if gated else 0
    g_ref, b_ref, o_ref, ob_ref = refs[pos:pos + 4]
    z = alpha * x_ref[...]
    for a, r in enumerate(add_refs):
        y = r[...]
        if gated:
            y = y * gate_ref[:, a:a + 1]
        z = z + y
    mu = jnp.mean(z, axis=-1, keepdims=True)
    zc = z - mu
    var = jnp.mean(zc * zc, axis=-1, keepdims=True)
    out = zc * lax.rsqrt(var + LN_EPS) * g_ref[...] + b_ref[...]
    o_ref[...] = out
    ob_ref[...] = out.astype(BF16)


def _residual_ln(x, adds, add_maps, gate, g, b, *, tm, name):
    m, d = x.shape
    gated = gate is not None
    in_specs = [pl.BlockSpec((tm, d), lambda i: (i, 0))]
    args = [x]
    for a, mp in zip(adds, add_maps):
        in_specs.append(pl.BlockSpec((tm, d), mp))
        args.append(a)
    if gated:
        in_specs.append(pl.BlockSpec((tm, gate.shape[1]), lambda i: (i, 0)))
        args.append(gate)
    in_specs += [pl.BlockSpec((1, d), lambda i: (0, 0)), pl.BlockSpec((1, d), lambda i: (0, 0))]
    args += [g.reshape(1, d), b.reshape(1, d)]
    return pl.pallas_call(
        functools.partial(_ln_body, alpha=ALPHA, n_add=len(adds), gated=gated),
        grid=(m // tm,),
        in_specs=in_specs,
        out_specs=[pl.BlockSpec((tm, d), lambda i: (i, 0)), pl.BlockSpec((tm, d), lambda i: (i, 0))],
        out_shape=[jax.ShapeDtypeStruct((m, d), F32), jax.ShapeDtypeStruct((m, d), BF16)],
        compiler_params=_params(("arbitrary",)),
        name=name,
    )(*args)


def _bias_body(tab_ref, dist_ref, o_ref, *, lo, hi):
    h = pl.program_id(0)
    d = dist_ref[...]
    n = jnp.maximum(d, 0)
    exact = REL_BUCKETS // 2
    nf = jnp.maximum(n, exact).astype(F32)
    large = exact + (jnp.log(nf * (1.0 / exact)) / math.log(REL_MAX_DIST / exact) * (REL_BUCKETS - exact)).astype(jnp.int32)
    bucket = jnp.where(n < exact, n, jnp.minimum(large, REL_BUCKETS - 1))
    acc = jnp.zeros(d.shape, F32)
    for bk in range(REL_BUCKETS):
        acc = jnp.where(bucket == bk, tab_ref[bk, h], acc)
    valid = (d >= lo) & (d <= hi)
    o_ref[0] = jnp.where(valid, acc, NEG_INF)


def _bias_table(rel_table, dist, lo, hi, name):
    r, c = dist.shape
    return pl.pallas_call(
        functools.partial(_bias_body, lo=lo, hi=hi),
        grid=(N_HEADS,),
        in_specs=[pl.BlockSpec(memory_space=pltpu.SMEM),
                  pl.BlockSpec((r, c), lambda h: (0, 0))],
        out_specs=pl.BlockSpec((1, r, c), lambda h: (h, 0, 0)),
        out_shape=jax.ShapeDtypeStruct((N_HEADS, r, c), F32),
        compiler_params=_params(("arbitrary",)),
        name=name,
    )(rel_table, jnp.asarray(dist, jnp.int32))


BIG = 1 << 30


def _dist_tables():
    t128 = np.arange(QT)[None, :]
    tS = np.arange(SEQ)[None, :]
    t8 = PAST_LEN + np.arange(DEC_SEQ)[:, None]
    d = {}
    d["p_cmp"] = (tS - (np.arange(SEQ // CMP_STRIDE)[:, None] * CMP_STRIDE + CMP_BLOCK - 1), 0, BIG)
    d["p_win"] = (t128 + A_WINDOW - np.arange(A_WINDOW + QT)[:, None], 0, A_WINDOW)
    d["p_sel"] = (np.concatenate([u * QT + t128 - np.arange(QT)[:, None] for u in range(3)], axis=0), 0, BIG)
    d["p_swa"] = (t128 + B_WINDOW - np.arange(B_WINDOW + QT)[:, None], 0, B_WINDOW)
    n_cmp_s = PAST_LEN // CMP_STRIDE
    d["s_cmp"] = (t8 - (np.arange(n_cmp_s)[None, :] * CMP_STRIDE + CMP_BLOCK - 1), 0, BIG)
    d["s_win"] = (t8 - (PAST_LEN - A_WINDOW + np.arange(A_WINDOW + 16)[None, :]), 0, A_WINDOW)
    d["s_last"] = (t8 - (PAST_LEN - PAGE_SIZE + np.arange(PAGE_SIZE)[None, :]), 0, BIG)
    d["s_new"] = (t8 - (PAST_LEN + np.arange(16)[None, :]), 0, BIG)
    d["s_swa"] = (t8 - (PAST_LEN - B_WINDOW + np.arange(B_WINDOW + 16)[None, :]), 0, B_WINDOW)
    return d


CMP_PAIRS = CMP_BLOCK // 2


def _compress_body(pid_ref, pages_ref, wpair_ref, w2_ref, pe_ref, w1_ref, b1_ref, o_ref, buf, sem, *, pc, nch, npg):
    b = pl.program_id(0)
    c = pl.program_id(1)
    step = b * nch + c
    nsteps = pl.num_programs(0) * nch
    slot = step % 2
    hb = PAGE_SIZE // CMP_STRIDE
    m = pc * hb

    def copies(bb, cc, sl):
        out = []
        for i in range(pc):
            pg = pid_ref[bb, cc * pc + i]
            out.append(pltpu.make_async_copy(pages_ref.at[pg], buf.at[sl, pl.ds(i * hb, hb)], sem.at[sl]))
        pg = pid_ref[bb, jnp.minimum(cc * pc + pc, npg - 1)]
        out.append(pltpu.make_async_copy(pages_ref.at[pg, pl.ds(0, 1)], buf.at[sl, pl.ds(pc * hb, 1)], sem.at[sl]))
        return out

    @pl.when(step == 0)
    def _():
        for cp in copies(b, c, slot):
            cp.start()

    @pl.when(step + 1 < nsteps)
    def _():
        wrap = c + 1 == nch
        nb = jnp.where(wrap, b + 1, b)
        nc = jnp.where(wrap, 0, c + 1)
        for cp in copies(nb, nc, 1 - slot):
            cp.start()

    for cp in copies(b, c, slot):
        cp.wait()

    for kv in range(2):
        pe_term = jnp.dot(pe_ref[kv], w1_ref[kv], preferred_element_type=F32)[0:1] + b1_ref[kv]
        pe2 = jnp.concatenate([pe_term, pe_term], axis=1)
        acc = jnp.zeros((2 * m, 2 * CMP_HIDDEN), F32)
        for q in range(CMP_PAIRS):
            rows = []
            for j in range(2):
                lanes = pl.ds(kv * 256 + j * LANES, LANES)
                a0 = buf[slot, pl.ds((2 * q) // CMP_STRIDE, m), (2 * q) % CMP_STRIDE, lanes]
                a1 = buf[slot, pl.ds((2 * q + 1) // CMP_STRIDE, m), (2 * q + 1) % CMP_STRIDE, lanes]
                rows.append(jnp.concatenate([a0, a1], axis=1))
            lhs = jnp.concatenate(rows, axis=0).astype(BF16)
            acc = acc + jnp.dot(lhs, wpair_ref[kv, q], preferred_element_type=F32)
        hid = _silu(acc + pe2)
        out = jnp.dot(hid.astype(BF16), w2_ref[kv], preferred_element_type=F32)
        o_ref[0, kv, 0] = out[:m].astype(o_ref.dtype)
        o_ref[0, kv, 1] = out[m:].astype(o_ref.dtype)


def _compress(page_ids, pages, wpair, w2bd, pe8, w1f, b1, *, pc, name):
    nb, npg = page_ids.shape
    pages = lax.optimization_barrier(pages).reshape(-1, PAGE_SIZE // CMP_STRIDE, CMP_STRIDE, A_KV_WIDTH)
    nch = npg // pc
    m = pc * (PAGE_SIZE // CMP_STRIDE)
    const = lambda nd: (lambda b, c, pid: (0,) * nd)
    grid_spec = pltpu.PrefetchScalarGridSpec(
        num_scalar_prefetch=1,
        grid=(nb, nch),
        in_specs=[pl.BlockSpec(memory_space=pl.ANY),
                  pl.BlockSpec(wpair.shape, const(4)),
                  pl.BlockSpec(w2bd.shape, const(3)),
                  pl.BlockSpec(pe8.shape, const(3)),
                  pl.BlockSpec(w1f.shape, const(3)),
                  pl.BlockSpec(b1.shape, const(3))],
        out_specs=pl.BlockSpec((1, 2, 2, m, LANES), lambda b, c, pid: (b, 0, 0, c, 0)),
        scratch_shapes=[pltpu.VMEM((2, m + 1, CMP_STRIDE, A_KV_WIDTH), F32),
                        pltpu.SemaphoreType.DMA((2,))],
    )
    return pl.pallas_call(
        functools.partial(_compress_body, pc=pc, nch=nch, npg=npg),
        grid_spec=grid_spec,
        out_shape=jax.ShapeDtypeStruct((nb, 2, 2, npg * (PAGE_SIZE // CMP_STRIDE), LANES), BF16),
        compiler_params=_params(("arbitrary", "arbitrary")),
        name=name,
    )(page_ids, pages, wpair, w2bd, pe8, w1f, b1)


def _select_mask(slc, tpos, blk, n_sel, top_n):
    avail = (blk * SEL_BLOCK <= tpos) & (blk < n_sel)
    lag = tpos // SEL_BLOCK - blk
    forced = (blk == 0) | ((lag >= 0) & (lag < N_LOCAL))
    slc = jnp.where(forced, FORCE, slc)
    slc = jnp.where(avail, slc, -FORCE)

    def one(jp, cnt):
        col = jnp.sum(jnp.where(blk == jp, slc, 0.0), axis=1, keepdims=True)
        ahead = (col > slc) | ((col == slc) & (blk > jp))
        return cnt + jnp.where(ahead, 1.0, 0.0)

    cnt = lax.fori_loop(0, n_sel, one, jnp.zeros(slc.shape, F32))
    return jnp.where(avail & (cnt < top_n), 1.0, 0.0)


def _overlap_matrix(n_cmp, n_sel, rows, cols):
    msel = np.zeros((rows, cols), np.float32)
    per = SEL_BLOCK // CMP_STRIDE
    for j in range(n_sel):
        for mm in range(per):
            for r in range(CMP_BLOCK // CMP_STRIDE):
                n = per * j + mm - r
                if 0 <= n < n_cmp:
                    msel[n, j] += 1.0
    return msel


def _select_mask_t(slc, tpos, blk, n_sel, top_n):
    avail = blk * SEL_BLOCK <= tpos
    lag = tpos // SEL_BLOCK - blk
    forced = (blk == 0) | ((lag >= 0) & (lag < N_LOCAL))
    slc = jnp.where(forced, FORCE, slc)
    slc = jnp.where(avail, slc, -FORCE)
    cnt = jnp.zeros(slc.shape, F32)
    for jp in range(n_sel):
        row = slc[jp:jp + 1, :]
        ahead = (row > slc) | ((row == slc) & (blk > jp))
        cnt = cnt + jnp.where(ahead, 1.0, 0.0)
    return jnp.where(avail & (cnt < top_n), 1.0, 0.0)


def _online_update(st, v, m_scr, l_scr, acc_scr):
    m_old = m_scr[...]
    m_new = jnp.maximum(m_old, jnp.max(st, axis=0, keepdims=True))
    a = jnp.exp(m_old - m_new)
    et = jnp.where(st > 0.5 * NEG_INF, jnp.exp(st - m_new), 0.0)
    l_scr[...] = a * l_scr[...] + jnp.sum(et, axis=0, keepdims=True)
    acc_scr[...] = a * acc_scr[...] + _tn_dot(v, et.astype(BF16))
    m_scr[...] = m_new


def _nsa_prompt_body(qt_ref, kc_ref, vc_ref, ks_ref, vs_ref, kw_ref, vw_ref, bc_ref, bw_ref, bt_ref, gl_ref,
                     mselt_ref, o_ref, sel_scr, m_scr, l_scr, acc_scr):
    i = pl.program_id(2)
    grp = A_GROUP
    n_sel = SEQ // SEL_BLOCK
    qt = qt_ref[0, 0, 0]

    bias_c = jnp.concatenate([bc_ref[0, g] for g in range(grp)], axis=1)
    p = _softmax_cols(jnp.dot(kc_ref[0, 0], qt, preferred_element_type=F32) + bias_c)
    oc = _tn_dot(vc_ref[0, 0], p.astype(BF16))

    score = p[:, 0:QT]
    for g in range(1, grp):
        score = score + p[:, g * QT:(g + 1) * QT]
    slc = jnp.dot(mselt_ref[...], score, precision=HIGHEST, preferred_element_type=F32)[0:n_sel]
    tpos = i * QT + lax.broadcasted_iota(jnp.int32, (n_sel, QT), 1)
    blk = lax.broadcasted_iota(jnp.int32, (n_sel, QT), 0)
    sel_scr[...] = _select_mask_t(slc, tpos, blk, n_sel, N_SELECT)

    m_scr[...] = jnp.full(m_scr.shape, NEG_INF, F32)
    l_scr[...] = jnp.zeros(l_scr.shape, F32)
    acc_scr[...] = jnp.zeros(acc_scr.shape, F32)
    half = lax.broadcasted_iota(jnp.int32, (QT, grp * QT), 0) < SEL_BLOCK
    per_tile = QT // SEL_BLOCK

    def sel_tile(jj, carry):
        start = pl.multiple_of(jj * QT, QT)
        u = jnp.minimum(i - jj, 2)
        st = jnp.dot(ks_ref[0, 0, pl.ds(start, QT), :], qt, preferred_element_type=F32) + bt_ref[0, u]
        r0 = sel_scr[pl.ds(per_tile * jj, 1), :]
        r1 = sel_scr[pl.ds(per_tile * jj + 1, 1), :]
        r0 = jnp.concatenate([r0] * grp, axis=1)
        r1 = jnp.concatenate([r1] * grp, axis=1)
        keep = jnp.where(half, r0, r1) > 0.5
        _online_update(jnp.where(keep, st, NEG_INF), vs_ref[0, 0, pl.ds(start, QT), :], m_scr, l_scr, acc_scr)
        return carry

    lax.fori_loop(0, i + 1, sel_tile, 0)
    osel = acc_scr[...] / jnp.maximum(l_scr[...], TINY)

    n_t = A_WINDOW // QT + 1
    parts = []
    starts = []
    for u in range(n_t):
        j = i - (n_t - 1) + u
        start = pl.multiple_of(jnp.maximum(j, 0) * QT, QT)
        starts.append(start)
        su = jnp.dot(kw_ref[0, 0, pl.ds(start, QT), :], qt, preferred_element_type=F32) + bw_ref[0, u * QT:(u + 1) * QT, :]
        parts.append(jnp.where(j >= 0, su, NEG_INF))
    pw = _softmax_cols(jnp.concatenate(parts, axis=0)).astype(BF16)
    ow = jnp.zeros((HEAD_DIM, grp * QT), F32)
    for u in range(n_t):
        ow = ow + _tn_dot(vw_ref[0, 0, pl.ds(starts[u], QT), :], pw[u * QT:(u + 1) * QT])

    gates = 1.0 / (1.0 + jnp.exp(-gl_ref[0, 0, 0]))
    o_ref[0, 0, 0] = (gates[0:1] * oc + gates[1:2] * osel + gates[2:3] * ow).astype(o_ref.dtype)


def _nsa_prompt(qt, kc, vc, ks, vs, kw, vw, bias_cmp, bias_win, bias_sel, gate_t):
    nq = SEQ // QT
    grp = A_GROUP
    cols = grp * QT
    n_cmp = SEQ // CMP_STRIDE
    mselt = jnp.asarray(_overlap_matrix(n_cmp - 1, SEQ // SEL_BLOCK, n_cmp, LANES).T.copy())
    kv_spec = lambda n: pl.BlockSpec((1, 1, n, HEAD_DIM), lambda b, k, i: (b, k, 0, 0))
    tile_spec = lambda r: pl.BlockSpec((1, 1, 1, r, cols), lambda b, k, i: (b, k, i, 0, 0))
    n_win = A_WINDOW + QT
    return pl.pallas_call(
        _nsa_prompt_body,
        grid=(BATCH, A_KV_HEADS, nq),
        in_specs=[tile_spec(HEAD_DIM),
                  kv_spec(n_cmp), kv_spec(n_cmp), kv_spec(SEQ), kv_spec(SEQ), kv_spec(SEQ), kv_spec(SEQ),
                  pl.BlockSpec((1, grp, n_cmp, QT), lambda b, k, i: (k, 0, 0, i)),
                  pl.BlockSpec((1, n_win, cols), lambda b, k, i: (k, 0, 0)),
                  pl.BlockSpec((1, 3, QT, cols), lambda b, k, i: (k, 0, 0, 0)),
                  tile_spec(3),
                  pl.BlockSpec((LANES, n_cmp), lambda b, k, i: (0, 0))],
        out_specs=tile_spec(HEAD_DIM),
        out_shape=jax.ShapeDtypeStruct((BATCH, A_KV_HEADS, nq, HEAD_DIM, cols), BF16),
        scratch_shapes=[pltpu.VMEM((SEQ // SEL_BLOCK, QT), F32),
                        pltpu.VMEM((1, cols), F32), pltpu.VMEM((1, cols), F32),
                        pltpu.VMEM((HEAD_DIM, cols), F32)],
        compiler_params=_params(("arbitrary", "arbitrary", "arbitrary")),
        name="nsa_prompt",
    )(qt, kc, vc, ks, vs, kw, vw, bias_cmp, bias_win, bias_sel, gate_t, mselt)


SEL_PAGES = 16
S_COLS = A_KV_HEADS * A_GROUP * DEC_SEQ
H_COLS = A_GROUP * DEC_SEQ
S_NSEL = -(-(PAST_LEN + DEC_SEQ) // SEL_BLOCK)
S_NSEL_PAD = 384


def _softmax_cols(s):
    mx = jnp.max(s, axis=0, keepdims=True)
    e = jnp.where(s > 0.5 * NEG_INF, jnp.exp(s - mx), 0.0)
    return e / jnp.maximum(jnp.sum(e, axis=0, keepdims=True), TINY)


def _nsa_sample_body(pid_ref, qt_ref, kc_ref, vc_ref, kw_ref, vw_ref, kn_ref, vn_ref, bc_ref, bw_ref, bl_ref, bn_ref,
                     cf_ref, gl_ref, gsum_ref, msel_ref, gexp_ref, eexp_ref, pages_ref, o_ref,
                     buf, sem, sel_scr, oc_scr, ow_scr, m_scr, l_scr, acc_scr, *, npg):
    b = pl.program_id(0)
    c = pl.program_id(1)
    nch = npg // SEL_PAGES
    step = b * nch + c
    nsteps = pl.num_programs(0) * nch
    slot = step % 2

    def copies(bb, cc, sl):
        return [pltpu.make_async_copy(pages_ref.at[pid_ref[bb, cc * SEL_PAGES + i]], buf.at[sl, i], sem.at[sl])
                for i in range(SEL_PAGES)]

    @pl.when(step == 0)
    def _():
        for cp in copies(b, c, slot):
            cp.start()

    @pl.when(step + 1 < nsteps)
    def _():
        wrap = c + 1 == nch
        for cp in copies(jnp.where(wrap, b + 1, b), jnp.where(wrap, 0, c + 1), 1 - slot):
            cp.start()

    qt = qt_ref[0]

    @pl.when(c == 0)
    def _():
        p = _softmax_cols(jnp.dot(kc_ref[0], qt, preferred_element_type=F32) + bc_ref[...])
        oc_scr[...] = _tn_dot(vc_ref[0], p.astype(BF16))
        score = lax.dot_general(gsum_ref[...], p, (((1,), (1,)), ((), ())), precision=HIGHEST,
                                preferred_element_type=F32)
        slc = jnp.dot(score, msel_ref[...], precision=HIGHEST, preferred_element_type=F32)
        r = lax.broadcasted_iota(jnp.int32, slc.shape, 0)
        tpos = PAST_LEN + r % DEC_SEQ
        blk = lax.broadcasted_iota(jnp.int32, slc.shape, 1)
        sel = _select_mask(slc, tpos, blk, S_NSEL, N_SELECT).astype(BF16)
        sel_scr[...] = _tn_dot(sel, gexp_ref[...])
        pw = _softmax_cols(jnp.dot(kw_ref[0], qt, preferred_element_type=F32) + bw_ref[...])
        ow_scr[...] = _tn_dot(vw_ref[0], pw.astype(BF16))
        m_scr[...] = jnp.full(m_scr.shape, NEG_INF, F32)
        l_scr[...] = jnp.zeros(l_scr.shape, F32)
        acc_scr[...] = jnp.zeros(acc_scr.shape, F32)

    for cp in copies(b, c, slot):
        cp.wait()

    nk = SEL_PAGES * PAGE_SIZE
    blocks = nk // SEL_BLOCK
    selc = sel_scr[pl.ds(pl.multiple_of(c * blocks, blocks), blocks), :].astype(BF16)
    keymask = jnp.dot(eexp_ref[...], selc, preferred_element_type=F32)
    far = jnp.broadcast_to(cf_ref[...], (PAGE_SIZE, S_COLS))
    last = jnp.where(c == nch - 1, bl_ref[...], far)
    for h in range(A_KV_HEADS):
        cs = slice(h * H_COLS, (h + 1) * H_COLS)
        ds_ = slice(h * HEAD_DIM, (h + 1) * HEAD_DIM)
        kt = jnp.concatenate([buf[slot, i, h] for i in range(SEL_PAGES)], axis=1).astype(BF16)
        vt = jnp.concatenate([buf[slot, i, A_KV_HEADS + h] for i in range(SEL_PAGES)], axis=1).astype(BF16)
        st = _tn_dot(kt, qt[ds_, cs])
        st = jnp.concatenate([st[:nk - PAGE_SIZE] + cf_ref[:, cs], st[nk - PAGE_SIZE:] + last[:, cs]], axis=0)
        st = jnp.where(keymask[:, cs] > 0.5, st, NEG_INF)
        m_old = m_scr[:, cs]
        m_new = jnp.maximum(m_old, jnp.max(st, axis=0, keepdims=True))
        a = jnp.exp(m_old - m_new)
        et = jnp.where(st > 0.5 * NEG_INF, jnp.exp(st - m_new), 0.0)
        l_scr[:, cs] = a * l_scr[:, cs] + jnp.sum(et, axis=0, keepdims=True)
        acc_scr[ds_, cs] = a * acc_scr[ds_, cs] + jnp.dot(vt, et.astype(BF16), preferred_element_type=F32)
        m_scr[:, cs] = m_new

    @pl.when(c == nch - 1)
    def _():
        sn = jnp.dot(kn_ref[0], qt, preferred_element_type=F32) + bn_ref[...]
        new_blk = PAST_LEN // SEL_BLOCK
        sn = jnp.where(sel_scr[new_blk:new_blk + 1, :] > 0.5, sn, NEG_INF)
        _online_update(sn, vn_ref[0], m_scr, l_scr, acc_scr)
        osel = acc_scr[...] / jnp.maximum(l_scr[...], TINY)
        gates = 1.0 / (1.0 + jnp.exp(-gl_ref[0]))
        o_ref[0] = gates[0:1] * oc_scr[...] + gates[1:2] * osel + gates[2:3] * ow_scr[...]


def _nsa_sample(page_table, qt, kc, vc, kwin, vwin, knew, vnew, b_cmp, b_win, b_last, b_new, c_far, gate_t, pages):
    nch = N_PAGES // SEL_PAGES
    n_cmp = PAST_LEN // CMP_STRIDE
    kwid = A_KV_HEADS * HEAD_DIM
    gsum = np.zeros((A_KV_HEADS * DEC_SEQ, S_COLS), np.float32)
    for k in range(A_KV_HEADS):
        for g in range(A_GROUP):
            for t in range(DEC_SEQ):
                gsum[k * DEC_SEQ + t, (k * A_GROUP + g) * DEC_SEQ + t] = 1.0
    msel = _overlap_matrix(n_cmp - 1, S_NSEL, n_cmp, S_NSEL_PAD)
    nk = SEL_PAGES * PAGE_SIZE
    eexp = np.zeros((nk, nk // SEL_BLOCK), np.float32)
    eexp[np.arange(nk), np.arange(nk) // SEL_BLOCK] = 1.0
    per_b = lambda shape: pl.BlockSpec((1,) + shape, lambda b, c, pid: (b, 0, 0))
    const2 = lambda shape: pl.BlockSpec(shape, lambda b, c, pid: (0, 0))
    nwin = kwin.shape[1]
    grid_spec = pltpu.PrefetchScalarGridSpec(
        num_scalar_prefetch=1,
        grid=(DEC_BATCH, nch),
        in_specs=[per_b((kwid, S_COLS)),
                  per_b((n_cmp, kwid)), per_b((n_cmp, kwid)),
                  per_b((nwin, kwid)), per_b((nwin, kwid)),
                  per_b((16, kwid)), per_b((16, kwid)),
                  const2((n_cmp, S_COLS)), const2((nwin, S_COLS)), const2((PAGE_SIZE, S_COLS)), const2((16, S_COLS)),
                  const2((1, S_COLS)), per_b((3, S_COLS)),
                  const2(gsum.shape), const2(msel.shape), const2(gsum.shape), const2(eexp.shape),
                  pl.BlockSpec(memory_space=pl.ANY)],
        out_specs=pl.BlockSpec((1, kwid, S_COLS), lambda b, c, pid: (b, 0, 0)),
        scratch_shapes=[pltpu.VMEM((2, SEL_PAGES, 2 * A_KV_HEADS, HEAD_DIM, PAGE_SIZE), F32),
                        pltpu.SemaphoreType.DMA((2,)),
                        pltpu.VMEM((S_NSEL_PAD, S_COLS), F32),
                        pltpu.VMEM((kwid, S_COLS), F32), pltpu.VMEM((kwid, S_COLS), F32),
                        pltpu.VMEM((1, S_COLS), F32), pltpu.VMEM((1, S_COLS), F32),
                        pltpu.VMEM((kwid, S_COLS), F32)],
    )
    return pl.pallas_call(
        functools.partial(_nsa_sample_body, npg=N_PAGES),
        grid_spec=grid_spec,
        out_shape=jax.ShapeDtypeStruct((DEC_BATCH, kwid, S_COLS), F32),
        compiler_params=_params(("arbitrary", "arbitrary")),
        name="nsa_sample",
    )(page_table, qt, kc, vc, kwin, vwin, knew, vnew, b_cmp, b_win, b_last, b_new, c_far, gate_t,
      jnp.asarray(gsum), jnp.asarray(msel), jnp.asarray(gsum, BF16), jnp.asarray(eexp, BF16), pages)


def _swa_prompt_body(qt_ref, k_ref, v_ref, bias_ref, sink_ref, o_ref):
    i = pl.program_id(2)
    qt = qt_ref[0, 0, 0]
    parts = []
    starts = []
    for u in range(2):
        j = i - 1 + u
        start = pl.multiple_of(jnp.maximum(j, 0) * QT, QT)
        starts.append(start)
        su = jnp.dot(k_ref[0, 0, pl.ds(start, QT), :], qt, preferred_element_type=F32) + bias_ref[0, u * QT:(u + 1) * QT, :]
        parts.append(jnp.where(j >= 0, su, NEG_INF))
    s = jnp.concatenate(parts, axis=0)
    sink = sink_ref[0]
    mx = jnp.maximum(jnp.max(s, axis=0, keepdims=True), sink)
    e = jnp.where(s > 0.5 * NEG_INF, jnp.exp(s - mx), 0.0)
    den = jnp.sum(e, axis=0, keepdims=True) + jnp.exp(sink - mx)
    p = (e / jnp.maximum(den, TINY)).astype(BF16)
    o = jnp.zeros((HEAD_DIM, qt.shape[1]), F32)
    for u in range(2):
        o = o + _tn_dot(v_ref[0, 0, pl.ds(starts[u], QT), :], p[u * QT:(u + 1) * QT])
    o_ref[0, 0, 0] = o.astype(o_ref.dtype)


def _swa_prompt(qt, k, v, bias, sink_cols):
    nq = SEQ // QT
    cols = B_GROUP * QT
    kv_spec = pl.BlockSpec((1, 1, SEQ, HEAD_DIM), lambda b, h, i: (b, h, 0, 0))
    tile = pl.BlockSpec((1, 1, 1, HEAD_DIM, cols), lambda b, h, i: (b, h, i, 0, 0))
    return pl.pallas_call(
        _swa_prompt_body,
        grid=(BATCH, B_KV_HEADS, nq),
        in_specs=[tile, kv_spec, kv_spec,
                  pl.BlockSpec((1, 2 * QT, cols), lambda b, h, i: (h, 0, 0)),
                  pl.BlockSpec((1, 1, cols), lambda b, h, i: (h, 0, 0))],
        out_specs=tile,
        out_shape=jax.ShapeDtypeStruct((BATCH, B_KV_HEADS, nq, HEAD_DIM, cols), BF16),
        compiler_params=_params(("arbitrary", "arbitrary", "arbitrary")),
        name="swa_prompt",
    )(qt, k, v, bias, sink_cols)


def _swa_sample_body(qt_ref, k_ref, v_ref, bias_ref, sink_ref, o_ref):
    s = jnp.dot(k_ref[0], qt_ref[0], preferred_element_type=F32) + bias_ref[...]
    sink = sink_ref[...]
    mx = jnp.maximum(jnp.max(s, axis=0, keepdims=True), sink)
    e = jnp.where(s > 0.5 * NEG_INF, jnp.exp(s - mx), 0.0)
    den = jnp.sum(e, axis=0, keepdims=True) + jnp.exp(sink - mx)
    p = e / jnp.maximum(den, TINY)
    o_ref[0] = _tn_dot(v_ref[0], p.astype(BF16))


def _swa_sample(qt, k, v, bias_t, sink_row):
    nk = k.shape[1]
    w = B_KV_HEADS * HEAD_DIM
    per_b = lambda shape: pl.BlockSpec((1,) + shape, lambda b: (b, 0, 0))
    return pl.pallas_call(
        _swa_sample_body,
        grid=(DEC_BATCH,),
        in_specs=[per_b((w, S_COLS)), per_b((nk, w)), per_b((nk, w)),
                  pl.BlockSpec((nk, S_COLS), lambda b: (0, 0)), pl.BlockSpec((1, S_COLS), lambda b: (0, 0))],
        out_specs=per_b((w, S_COLS)),
        out_shape=jax.ShapeDtypeStruct((DEC_BATCH, w, S_COLS), F32),
        compiler_params=_params(("arbitrary",)),
        name="swa_sample",
    )(qt, k, v, bias_t, sink_row)


MOE_TM = 512
MOE_TILES = -(-(TOP_K * MT + N_EXPERTS * (MOE_TM - 1)) // MOE_TM)
MOE_ROWS = MOE_TILES * MOE_TM
MOE_TN_UP = 512
MOE_TN_DOWN = 1024
MOE_SUB = 512
MOE_KSPLIT = 4
MOE_GATHER_ROWS = 256


def _router_body(x_ref, w_ref, b_ref, o_ref):
    logits = jnp.dot(x_ref[...], w_ref[...], precision=HIGHEST, preferred_element_type=F32) + b_ref[...]
    lane = lax.broadcasted_iota(jnp.int32, logits.shape, 1)
    logits = jnp.where(lane < N_EXPERTS, logits, -jnp.inf)
    v1 = jnp.max(logits, axis=1, keepdims=True)
    i1 = jnp.min(jnp.where(logits == v1, lane, LANES), axis=1, keepdims=True)
    rest = jnp.where(lane == i1, -jnp.inf, logits)
    v2 = jnp.max(rest, axis=1, keepdims=True)
    i2 = jnp.min(jnp.where(rest == v2, lane, LANES), axis=1, keepdims=True)
    e2 = jnp.exp(v2 - v1)
    den = 1.0 + e2
    out = jnp.where(lane == 0, i1.astype(F32),
                    jnp.where(lane == 1, i2.astype(F32),
                              jnp.where(lane == 2, 1.0 / den, jnp.where(lane == 3, e2 / den, 0.0))))
    o_ref[...] = out


def _router(x, w, b, *, tm):
    m, d = x.shape
    wp = jnp.zeros((d, LANES), F32).at[:, :N_EXPERTS].set(w)
    bp = jnp.zeros((1, LANES), F32).at[0, :N_EXPERTS].set(b)
    return pl.pallas_call(
        _router_body,
        grid=(m // tm,),
        in_specs=[pl.BlockSpec((tm, d), lambda i: (i, 0)), pl.BlockSpec((d, LANES), lambda i: (0, 0)),
                  pl.BlockSpec((1, LANES), lambda i: (0, 0))],
        out_specs=pl.BlockSpec((tm, LANES), lambda i: (i, 0)),
        out_shape=jax.ShapeDtypeStruct((m, LANES), F32),
        compiler_params=_params(("arbitrary",)),
        name="moe_router",
    )(x, wp, bp)


def _gather_body(idx_ref, src_ref, o_ref, *scratch, rows, staged):
    if staged:
        stage, sem = scratch
    else:
        (sem,) = scratch
        stage = o_ref
    base = pl.program_id(0) * rows

    def row_copy(r, src_row):
        return pltpu.make_async_copy(src_ref.at[pl.ds(src_row, 1)], stage.at[pl.ds(r, 1)], sem)

    def issue(r, carry):
        row_copy(r, idx_ref[base + r]).start()
        return carry

    def drain(r, carry):
        row_copy(r, 0).wait()
        return carry

    lax.fori_loop(0, rows, issue, 0)
    lax.fori_loop(0, rows, drain, 0)
    if staged:
        o_ref[...] = stage[...].astype(o_ref.dtype)


def _gather_rows(src, idx, *, rows, out_dtype, name):
    n = idx.shape[0]
    w = src.shape[1]
    staged = out_dtype != src.dtype
    scratch = ([pltpu.VMEM((rows, w), src.dtype)] if staged else []) + [pltpu.SemaphoreType.DMA(())]
    grid_spec = pltpu.PrefetchScalarGridSpec(
        num_scalar_prefetch=1,
        grid=(n // rows,),
        in_specs=[pl.BlockSpec(memory_space=pl.ANY)],
        out_specs=pl.BlockSpec((rows, w), lambda i, idx_ref: (i, 0)),
        scratch_shapes=scratch,
    )
    return pl.pallas_call(
        functools.partial(_gather_body, rows=rows, staged=staged),
        grid_spec=grid_spec,
        out_shape=jax.ShapeDtypeStruct((n, w), out_dtype),
        compiler_params=_params(("arbitrary",)),
        name=name,
    )(idx, src)


def _moe_up_body(te_ref, nv_ref, x_ref, wg_ref, wu_ref, o_ref, wg_b, wu_b):
    i = pl.program_id(1)
    e = te_ref[i]
    prev = te_ref[jnp.maximum(i - 1, 0)]

    @pl.when((i == 0) | (e != prev))
    def _():
        wg_b[...] = wg_ref[0].astype(BF16)
        wu_b[...] = wu_ref[0].astype(BF16)

    @pl.when(i < nv_ref[0])
    def _():
        x = x_ref[...]
        g = jnp.dot(x, wg_b[...], preferred_element_type=F32)
        u = jnp.dot(x, wu_b[...], preferred_element_type=F32)
        o_ref[...] = (_silu(g) * u).astype(o_ref.dtype)

    @pl.when(i >= nv_ref[0])
    def _():
        o_ref[...] = jnp.zeros(o_ref.shape, o_ref.dtype)


def _moe_up(tile_expert, n_valid, xs, w_in):
    d = xs.shape[1]
    f = w_in.shape[2] // 2
    nb = f // MOE_TN_UP
    grid_spec = pltpu.PrefetchScalarGridSpec(
        num_scalar_prefetch=2,
        grid=(nb, MOE_TILES),
        in_specs=[pl.BlockSpec((MOE_TM, d), lambda j, i, te, nv: (i, 0)),
                  pl.BlockSpec((1, d, MOE_TN_UP), lambda j, i, te, nv: (te[i], 0, j)),
                  pl.BlockSpec((1, d, MOE_TN_UP), lambda j, i, te, nv: (te[i], 0, j + nb))],
        out_specs=pl.BlockSpec((MOE_TM, MOE_TN_UP), lambda j, i, te, nv: (i, j)),
        scratch_shapes=[pltpu.VMEM((d, MOE_TN_UP), BF16), pltpu.VMEM((d, MOE_TN_UP), BF16)],
    )
    return pl.pallas_call(
        _moe_up_body,
        grid_spec=grid_spec,
        out_shape=jax.ShapeDtypeStruct((MOE_ROWS, f), BF16),
        compiler_params=_params(("arbitrary", "arbitrary")),
        name="moe_up",
    )(tile_expert, n_valid, xs, w_in, w_in)


def _moe_down_body(te_ref, nv_ref, *refs, first):
    if first:
        x_ref, w_ref, o_ref, w_b = refs
        prev_ref = None
    else:
        x_ref, w_ref, prev_ref, o_ref, w_b = refs
    i = pl.program_id(1)
    per = MOE_TM // MOE_SUB
    e = te_ref[i // per]
    prev = te_ref[jnp.maximum(i - 1, 0) // per]

    @pl.when((i == 0) | (e != prev))
    def _():
        w_b[...] = w_ref[0].astype(BF16)

    @pl.when(i // per < nv_ref[0])
    def _():
        acc = jnp.dot(x_ref[...], w_b[...], preferred_element_type=F32)
        o_ref[...] = acc if first else acc + prev_ref[...]

    @pl.when(i // per >= nv_ref[0])
    def _():
        o_ref[...] = jnp.zeros(o_ref.shape, o_ref.dtype)


def _moe_down(tile_expert, n_valid, h, w_out, prev, *, part):
    f = h.shape[1]
    d = w_out.shape[2]
    tk = f // MOE_KSPLIT
    per = MOE_TM // MOE_SUB
    first = prev is None
    in_specs = [pl.BlockSpec((MOE_SUB, tk), lambda j, i, te, nv: (i, part)),
                pl.BlockSpec((1, tk, MOE_TN_DOWN), lambda j, i, te, nv: (te[i // per], part, j))]
    args = [h, w_out]
    aliases = {}
    if not first:
        in_specs.append(pl.BlockSpec((MOE_SUB, MOE_TN_DOWN), lambda j, i, te, nv: (i, j)))
        args.append(prev)
        aliases = {4: 0}
    grid_spec = pltpu.PrefetchScalarGridSpec(
        num_scalar_prefetch=2,
        grid=(d // MOE_TN_DOWN, MOE_TILES * per),
        in_specs=in_specs,
        out_specs=pl.BlockSpec((MOE_SUB, MOE_TN_DOWN), lambda j, i, te, nv: (i, j)),
        scratch_shapes=[pltpu.VMEM((tk, MOE_TN_DOWN), BF16)],
    )
    return pl.pallas_call(
        functools.partial(_moe_down_body, first=first),
        grid_spec=grid_spec,
        out_shape=jax.ShapeDtypeStruct((MOE_ROWS, d), F32),
        input_output_aliases=aliases,
        compiler_params=_params(("arbitrary", "arbitrary")),
        name=f"moe_down_{part}",
    )(tile_expert, n_valid, *args)


def _route(top):
    idx = top[:, 0:TOP_K].astype(jnp.int32)
    gate = top[:, TOP_K:2 * TOP_K]
    flat_e = idx.reshape(-1)
    onehot = (flat_e[:, None] == jnp.arange(N_EXPERTS)[None, :]).astype(jnp.int32)
    csum = jnp.cumsum(onehot, axis=0)
    counts = csum[-1]
    rank = jnp.take_along_axis(csum, flat_e[:, None], axis=1)[:, 0] - 1
    tiles_per = (counts + MOE_TM - 1) // MOE_TM
    tile_end = jnp.cumsum(tiles_per)
    tile_start = tile_end - tiles_per
    pos = tile_start[flat_e] * MOE_TM + rank
    src_tok = jnp.zeros((MOE_ROWS,), jnp.int32).at[pos].set(jnp.arange(TOP_K * MT, dtype=jnp.int32) // TOP_K)
    n_valid = tile_end[-1]
    tile_ids = jnp.arange(MOE_TILES, dtype=jnp.int32)
    tile_expert = jnp.searchsorted(tile_end, jnp.minimum(tile_ids, n_valid - 1), side="right").astype(jnp.int32)
    tile_expert = jnp.minimum(tile_expert, N_EXPERTS - 1)
    return gate, pos.reshape(MT, TOP_K), src_tok, tile_expert, n_valid.reshape(1).astype(jnp.int32)


def _moe(x_f32, w_router, b_router, w_in, w_out):
    top = _router(x_f32, w_router, b_router, tm=264)
    gate, pos, src_tok, tile_expert, n_valid = _route(top)
    xs = _gather_rows(x_f32, src_tok, rows=MOE_GATHER_ROWS, out_dtype=BF16, name="moe_dispatch")
    h = _moe_up(tile_expert, n_valid, xs, w_in)
    y = None
    for part in range(MOE_KSPLIT):
        y = _moe_down(tile_expert, n_valid, h, w_out, y, part=part)
    back = jnp.concatenate([pos[:, 0], pos[:, 1]])
    yy = _gather_rows(y, back, rows=MOE_GATHER_ROWS, out_dtype=F32, name="moe_collect")
    return yy, gate


def _tile_q(q, kvh, grp):
    nq = SEQ // QT
    q6 = q.reshape(BATCH, nq, QT, kvh, grp, HEAD_DIM)
    return q6.transpose(0, 3, 1, 5, 4, 2).reshape(BATCH, kvh, nq, HEAD_DIM, grp * QT)


def _untile_o(ot, kvh, grp):
    nq = SEQ // QT
    o6 = ot.reshape(BATCH, kvh, nq, HEAD_DIM, grp, QT)
    return o6.transpose(0, 2, 5, 1, 4, 3).reshape(MP, kvh * grp * HEAD_DIM)


def _tile_cols(tab, kvh, grp):
    rows = tab.shape[1]
    return tab.reshape(kvh, grp, rows, QT).transpose(0, 2, 1, 3).reshape(kvh, rows, grp * QT)


def _block_q(q, kvh, grp):
    q5 = q.reshape(DEC_BATCH, DEC_SEQ, kvh, grp, HEAD_DIM)
    eye = jnp.eye(kvh, dtype=q.dtype)
    qt = jnp.einsum("btkgd,kl->bldkgt", q5, eye)
    return qt.reshape(DEC_BATCH, kvh * HEAD_DIM, kvh * grp * DEC_SEQ)


def _unblock_o(ot, kvh, grp):
    o6 = ot.reshape(DEC_BATCH, kvh, HEAD_DIM, kvh, grp, DEC_SEQ)
    o = jnp.einsum("bkdkgt->btkgd", o6)
    return o.reshape(MS, kvh * grp * HEAD_DIM)


def _cols_t(tab):
    return tab.transpose(2, 0, 1).reshape(tab.shape[2], N_HEADS * DEC_SEQ)


def kernel(x_prompt, x_sample, cache_cmp_kv, cache_sel_kv, state_win_kv, state_shared_kv, page_table, rel_table, ln_g, ln_b, a_w_in, a_cmp_pe, a_cmp_w1, a_cmp_b1, a_cmp_w2, a_w_out, b_w_kv, b_w_q, b_sinks, b_w_out, dense_w_in, dense_w_out, moe_router_w, moe_router_b, moe_w_in, moe_w_out):
    n_phys = cache_cmp_kv.shape[1]
    x0 = jnp.concatenate([x_prompt.reshape(MP, D_MODEL), x_sample.reshape(MS, D_MODEL)], axis=0)
    x0b = x0.astype(BF16)

    dist = _dist_tables()
    tabs = {k: _bias_table(rel_table, d, lo, hi, "bias_" + k) for k, (d, lo, hi) in dist.items()}
    c_far = jnp.repeat(rel_table[REL_BUCKETS - 1], DEC_SEQ).reshape(1, N_HEADS * DEC_SEQ)

    la = 0
    w_in = a_w_in[la]
    kvg_w = jnp.pad(w_in[:, Q_WIDTH:], ((0, 0), (0, 64))).astype(BF16)
    q = _matmul(x0b, w_in[:, :Q_WIDTH].astype(BF16), tm=TM, tn=1024, out_dtype=BF16, scale=SCALE, name="a_q_proj")
    hk = _matmul(x0b, kvg_w, tm=TM, tn=896, out_dtype=F32, name="a_kv_proj")
    kvw = A_KV_WIDTH
    kv_c, kv_s, kv_w = hk[:, 0:kvw], hk[:, kvw:2 * kvw], hk[:, 2 * kvw:3 * kvw]
    gate_logits = hk[:, 3 * kvw:3 * kvw + 3 * N_HEADS]

    w1 = a_cmp_w1[la]
    eye2 = jnp.eye(2, dtype=F32)
    wpair = jnp.einsum("cqpdh,kl->cqpkdlh", w1.reshape(2, CMP_PAIRS, 2, HEAD_DIM, CMP_HIDDEN), eye2)
    wpair = wpair.reshape(2, CMP_PAIRS, 4 * HEAD_DIM, 2 * CMP_HIDDEN).astype(BF16)
    w2bd = jnp.einsum("chd,kl->ckhld", a_cmp_w2[la], eye2).reshape(2, 2 * CMP_HIDDEN, 2 * HEAD_DIM).astype(BF16)
    pe8 = jnp.broadcast_to(a_cmp_pe[la].reshape(2, 1, CMP_BLOCK * HEAD_DIM), (2, 8, CMP_BLOCK * HEAD_DIM)).astype(BF16)
    w1f = w1.reshape(2, CMP_BLOCK * HEAD_DIM, CMP_HIDDEN).astype(BF16)
    b1 = a_cmp_b1[la].reshape(2, 1, CMP_HIDDEN)

    pages_per_seq = SEQ // PAGE_SIZE
    prompt_ids = jnp.arange(BATCH * pages_per_seq, dtype=jnp.int32).reshape(BATCH, pages_per_seq)
    ckv_p = _compress(prompt_ids, kv_c[:MP].reshape(BATCH * pages_per_seq, PAGE_SIZE, kvw), wpair, w2bd, pe8, w1f, b1,
                      pc=pages_per_seq, name="cmp_prompt")
    sample_ids = la * n_phys + page_table
    ckv_s = _compress(sample_ids, cache_cmp_kv.reshape(-1, PAGE_SIZE, kvw), wpair, w2bd, pe8, w1f, b1,
                      pc=32, name="cmp_sample")

    n_cmp_p = SEQ // CMP_STRIDE
    ckv_ph = ckv_p.reshape(BATCH, 2, 2, n_cmp_p, 2, HEAD_DIM).transpose(0, 1, 2, 4, 3, 5).reshape(BATCH, 2, A_KV_HEADS, n_cmp_p, HEAD_DIM)

    def kv_heads(kv):
        kv5 = kv.reshape(BATCH, SEQ, 2, A_KV_HEADS, HEAD_DIM).astype(BF16).transpose(2, 0, 3, 1, 4)
        return kv5[0], kv5[1]

    ks_p, vs_p = kv_heads(kv_s[:MP])
    kw_p, vw_p = kv_heads(kv_w[:MP])
    grp = A_GROUP
    nq = SEQ // QT
    bias_cmp = tabs["p_cmp"].reshape(A_KV_HEADS, grp, n_cmp_p, SEQ)
    bias_win = _tile_cols(tabs["p_win"], A_KV_HEADS, grp)
    bias_sel = _tile_cols(tabs["p_sel"], A_KV_HEADS, grp).reshape(A_KV_HEADS, 3, QT, grp * QT)
    gl_p = gate_logits[:MP].reshape(BATCH, nq, QT, 3, A_KV_HEADS, grp).transpose(0, 4, 1, 3, 5, 2).reshape(BATCH, A_KV_HEADS, nq, 3, grp * QT)
    ot_p = _nsa_prompt(_tile_q(q[:MP], A_KV_HEADS, grp), ckv_ph[:, 0], ckv_ph[:, 1], ks_p, vs_p, kw_p, vw_p,
                       bias_cmp, bias_win, bias_sel, gl_p)
    o_ap = _untile_o(ot_p, A_KV_HEADS, grp)

    q_s = q[MP:].reshape(DEC_BATCH, DEC_SEQ, Q_WIDTH)
    qt_a = _block_q(q_s, A_KV_HEADS, A_GROUP)
    n_cmp_s = PAST_LEN // CMP_STRIDE
    ckv_sa = ckv_s.reshape(DEC_BATCH, 2, 2, n_cmp_s, 2 * HEAD_DIM).transpose(0, 1, 3, 2, 4).reshape(DEC_BATCH, 2, n_cmp_s, 4 * HEAD_DIM)
    kv_w_new = kv_w[MP:].reshape(DEC_BATCH, DEC_SEQ, kvw)
    win_full = jnp.concatenate([state_win_kv[la].reshape(DEC_BATCH, A_WINDOW, kvw), kv_w_new], axis=1)
    win_pad = jnp.pad(win_full, ((0, 0), (0, 8), (0, 0))).astype(BF16)
    kv_s_new = kv_s[MP:].reshape(DEC_BATCH, DEC_SEQ, kvw)
    new_pad = jnp.pad(kv_s_new, ((0, 0), (0, 16 - DEC_SEQ), (0, 0))).astype(BF16)
    half = kvw // 2
    gl_s = gate_logits[MP:].reshape(DEC_BATCH, DEC_SEQ, 3, N_HEADS).transpose(0, 2, 3, 1).reshape(DEC_BATCH, 3, N_HEADS * DEC_SEQ)
    sel_pages = cache_sel_kv.reshape(-1, PAGE_SIZE, 2 * A_KV_HEADS, HEAD_DIM).transpose(0, 2, 3, 1)
    ot_a = _nsa_sample(sample_ids, qt_a, ckv_sa[:, 0], ckv_sa[:, 1], win_pad[..., :half], win_pad[..., half:],
                       new_pad[..., :half], new_pad[..., half:],
                       _cols_t(tabs["s_cmp"]), _cols_t(tabs["s_win"]), _cols_t(tabs["s_last"]), _cols_t(tabs["s_new"]),
                       c_far, gl_s, sel_pages)
    o_as = _unblock_o(ot_a, A_KV_HEADS, A_GROUP).astype(BF16)

    o_a = jnp.concatenate([o_ap, o_as], axis=0)
    y = _matmul(o_a, a_w_out[la].astype(BF16), tm=TM, tn=1024, out_dtype=F32, name="a_out_proj")
    row_map = lambda i: (i, 0)
    x1, x1b = _residual_ln(x0, [y], [row_map], None, ln_g[0, 0], ln_b[0, 0], tm=192, name="ln_0a")

    hmid = _matmul_swiglu(x1b, dense_w_in[0].astype(BF16), tm=TM, tn=256, name="dense_up")
    f = _matmul(hmid, dense_w_out[0].astype(BF16), tm=528, tn=512, out_dtype=F32, name="dense_down")
    x2, x2b = _residual_ln(x1, [f], [row_map], None, ln_g[0, 1], ln_b[0, 1], tm=192, name="ln_0b")

    sh = _matmul(x2b, b_w_kv.astype(BF16), tm=TM, tn=1024, out_dtype=F32, name="b_kv_proj")
    q1 = _matmul(x2b, b_w_q[0].astype(BF16), tm=TM, tn=1024, out_dtype=BF16, scale=SCALE, name="b_q_proj")
    shw = 2 * B_KV_HEADS * HEAD_DIM
    sh_p = sh[:MP].reshape(BATCH, SEQ, 2, B_KV_HEADS, HEAD_DIM)
    sh_new = sh[MP:].reshape(DEC_BATCH, DEC_SEQ, shw)
    sh5 = sh_p.astype(BF16).transpose(2, 0, 3, 1, 4)
    bias_swa = _tile_cols(tabs["p_swa"], B_KV_HEADS, B_GROUP)
    sink_cols = jnp.repeat(b_sinks[0], QT).reshape(B_KV_HEADS, 1, B_GROUP * QT)
    ot_bp = _swa_prompt(_tile_q(q1[:MP], B_KV_HEADS, B_GROUP), sh5[0], sh5[1], bias_swa, sink_cols)
    o_bp = _untile_o(ot_bp, B_KV_HEADS, B_GROUP)

    sh_full = jnp.concatenate([state_shared_kv.reshape(DEC_BATCH, B_WINDOW, shw), sh_new], axis=1)
    sh_pad = jnp.pad(sh_full, ((0, 0), (0, 8), (0, 0))).astype(BF16)
    qt_b = _block_q(q1[MP:].reshape(DEC_BATCH, DEC_SEQ, Q_WIDTH), B_KV_HEADS, B_GROUP)
    sink_row = jnp.repeat(b_sinks[0], DEC_SEQ).reshape(1, N_HEADS * DEC_SEQ)
    ot_b = _swa_sample(qt_b, sh_pad[..., :shw // 2], sh_pad[..., shw // 2:], _cols_t(tabs["s_swa"]), sink_row)
    o_bs = _unblock_o(ot_b, B_KV_HEADS, B_GROUP).astype(BF16)
    o_b = jnp.concatenate([o_bp, o_bs], axis=0)
    y = _matmul(o_b, b_w_out[0].astype(BF16), tm=TM, tn=1024, out_dtype=F32, name="b_out_proj")
    x3, _ = _residual_ln(x2, [y], [row_map], None, ln_g[1, 0], ln_b[1, 0], tm=192, name="ln_1a")

    yy, gate = _moe(x3, moe_router_w[0], moe_router_b[0], moe_w_in[0], moe_w_out[0])
    nblk = MT // 192
    x4, _ = _residual_ln(x3, [yy, yy], [row_map, lambda i: (i + nblk, 0)], gate, ln_g[1, 1], ln_b[1, 1], tm=192, name="ln_1b")

    y_prompt = x4[:MP].reshape(BATCH, SEQ, D_MODEL)
    y_sample = x4[MP:].reshape(DEC_BATCH, DEC_SEQ, D_MODEL)
    kv6 = lambda a, bsz, t: a.reshape(1, bsz, t, 2, A_KV_HEADS, HEAD_DIM)
    new_cmp_p = kv6(kv_c[:MP], BATCH, SEQ)
    new_cmp_s = kv6(kv_c[MP:], DEC_BATCH, DEC_SEQ)
    new_sel_p = kv6(kv_s[:MP], BATCH, SEQ)
    new_sel_s = kv6(kv_s[MP:], DEC_BATCH, DEC_SEQ)
    new_win_p = kv6(kv_w[:MP], BATCH, SEQ)[:, :, SEQ - A_WINDOW:]
    new_win_s = kv6(win_full[:, DEC_SEQ:], DEC_BATCH, A_WINDOW)
    new_sh_p = sh_p[:, SEQ - B_WINDOW:]
    new_sh_s = sh_full[:, DEC_SEQ:].reshape(DEC_BATCH, B_WINDOW, 2, B_KV_HEADS, HEAD_DIM)
    return (y_prompt, y_sample, new_cmp_p, new_cmp_s, new_sel_p, new_sel_s, new_win_p, new_win_s, new_sh_p, new_sh_s)
```

```python
import functools
import math

import jax
import jax.numpy as jnp
import numpy as np
from jax import lax
from jax.experimental import pallas as pl
from jax.experimental.pallas import tpu as pltpu

D_MODEL = 4096
BATCH = 4
SEQ = 2048
DEPTH = 2
DEC_BATCH = 32
DEC_SEQ = 8
PAST_LEN = 16384
PAGE_SIZE = 128
N_HEADS = 64
HEAD_DIM = 64
Q_WIDTH = N_HEADS * HEAD_DIM
SCALE = HEAD_DIM ** -0.5
A_KV_HEADS = 4
A_GROUP = N_HEADS // A_KV_HEADS
A_KV_WIDTH = 2 * A_KV_HEADS * HEAD_DIM
CMP_BLOCK = 32
CMP_STRIDE = 16
CMP_HIDDEN = 256
SEL_BLOCK = 64
N_SELECT = 16
N_LOCAL = 2
A_WINDOW = 512
B_KV_HEADS = 8
B_GROUP = N_HEADS // B_KV_HEADS
B_WINDOW = 128
REL_BUCKETS = 32
REL_MAX_DIST = 128
D_FF = 11008
N_EXPERTS = 8
TOP_K = 2
D_FF_EXPERT = 14336
ALPHA = (2 * DEPTH) ** 0.25
LN_EPS = 1e-5
NEG_INF = -1e30
TINY = 1e-30
FORCE = 1e9

LANES = 128
MP = BATCH * SEQ
MS = DEC_BATCH * DEC_SEQ
MT = MP + MS
TM = 1056
QT = 128
N_PAGES = PAST_LEN // PAGE_SIZE
VMEM_LIMIT = 56 * 1024 * 1024

F32 = jnp.float32
BF16 = jnp.bfloat16
HIGHEST = lax.Precision.HIGHEST


def _params(sem, vmem=VMEM_LIMIT):
    return pltpu.CompilerParams(dimension_semantics=sem, vmem_limit_bytes=vmem)


def _silu(x):
    return x * (1.0 / (1.0 + jnp.exp(-x)))


def _nt_dot(a, b):
    return lax.dot_general(a, b, (((1,), (1,)), ((), ())), preferred_element_type=F32)


def _tn_dot(a, b):
    return lax.dot_general(a, b, (((0,), (0,)), ((), ())), preferred_element_type=F32)


def _mm_body(x_ref, w_ref, o_ref, *, scale):
    acc = jnp.dot(x_ref[...], w_ref[...], preferred_element_type=F32)
    if scale != 1.0:
        acc = acc * scale
    o_ref[...] = acc.astype(o_ref.dtype)


def _matmul(x, w, *, tm, tn, out_dtype, scale=1.0, name):
    m, k = x.shape
    n = w.shape[1]
    return pl.pallas_call(
        functools.partial(_mm_body, scale=scale),
        grid=(n // tn, m // tm),
        in_specs=[pl.BlockSpec((tm, k), lambda j, i: (i, 0)),
                  pl.BlockSpec((k, tn), lambda j, i: (0, j))],
        out_specs=pl.BlockSpec((tm, tn), lambda j, i: (i, j)),
        out_shape=jax.ShapeDtypeStruct((m, n), out_dtype),
        compiler_params=_params(("arbitrary", "arbitrary")),
        name=name,
    )(x, w)


def _mm_swiglu_body(x_ref, wg_ref, wu_ref, o_ref):
    x = x_ref[...]
    g = jnp.dot(x, wg_ref[...], preferred_element_type=F32)
    u = jnp.dot(x, wu_ref[...], preferred_element_type=F32)
    o_ref[...] = (_silu(g) * u).astype(o_ref.dtype)


def _matmul_swiglu(x, w_in, *, tm, tn, name):
    m, k = x.shape
    f = w_in.shape[1] // 2
    nb = f // tn
    return pl.pallas_call(
        _mm_swiglu_body,
        grid=(m // tm, nb),
        in_specs=[pl.BlockSpec((tm, k), lambda i, j: (i, 0)),
                  pl.BlockSpec((k, tn), lambda i, j: (0, j)),
                  pl.BlockSpec((k, tn), lambda i, j: (0, j + nb))],
        out_specs=pl.BlockSpec((tm, tn), lambda i, j: (i, j)),
        out_shape=jax.ShapeDtypeStruct((m, f), BF16),
        compiler_params=_params(("arbitrary", "arbitrary")),
        name=name,
    )(x, w_in, w_in)


def _ln_body(*refs, alpha, n_add, gated):
    x_ref = refs[0]
    add_refs = refs[1:1 + n_add]
    pos = 1 + n_add
    gate_ref = refs[pos] if gated else None
    pos += 1 if gated else 0
    g_ref, b_ref, o_ref, ob_ref = refs[pos:pos + 4]
    z = alpha * x_ref[...]
    for a, r in enumerate(add_refs):
        y = r[...]
        if gated:
            y = y * gate_ref[:, a:a + 1]
        z = z + y
    mu = jnp.mean(z, axis=-1, keepdims=True)
    zc = z - mu
    var = jnp.mean(zc * zc, axis=-1, keepdims=True)
    out = zc * lax.rsqrt(var + LN_EPS) * g_ref[...] + b_ref[...]
    o_ref[...] = out
    ob_ref[...] = out.astype(BF16)


def _residual_ln(x, adds, add_maps, gate, g, b, *, tm, name):
    m, d = x.shape
    gated = gate is not None
    in_specs = [pl.BlockSpec((tm, d), lambda i: (i, 0))]
    args = [x]
    for a, mp in zip(adds, add_maps):
        in_specs.append(pl.BlockSpec((tm, d), mp))
        args.append(a)
    if gated:
        in_specs.append(pl.BlockSpec((tm, gate.shape[1]), lambda i: (i, 0)))
        args.append(gate)
    in_specs += [pl.BlockSpec((1, d), lambda i: (0, 0)), pl.BlockSpec((1, d), lambda i: (0, 0))]
    args += [g.reshape(1, d), b.reshape(1, d)]
    return pl.pallas_call(
        functools.partial(_ln_body, alpha=ALPHA, n_add=len(adds), gated=gated),
        grid=(m // tm,),
        in_specs=in_specs,
        out_specs=[pl.BlockSpec((tm, d), lambda i: (i, 0)), pl.BlockSpec((tm, d), lambda i: (i, 0))],
        out_shape=[jax.ShapeDtypeStruct((m, d), F32), jax.ShapeDtypeStruct((m, d), BF16)],
        compiler_params=_params(("arbitrary",)),
        name=name,
    )(*args)


def _bias_body(tab_ref, dist_ref, o_ref, *, lo, hi):
    h = pl.program_id(0)
    d = dist_ref[...]
    n = jnp.maximum(d, 0)
    exact = REL_BUCKETS // 2
    nf = jnp.maximum(n, exact).astype(F32)
    large = exact + (jnp.log(nf * (1.0 / exact)) / math.log(REL_MAX_DIST / exact) * (REL_BUCKETS - exact)).astype(jnp.int32)
    bucket = jnp.where(n < exact, n, jnp.minimum(large, REL_BUCKETS - 1))
    acc = jnp.zeros(d.shape, F32)
    for bk in range(REL_BUCKETS):
        acc = jnp.where(bucket == bk, tab_ref[bk, h], acc)
    valid = (d >= lo) & (d <= hi)
    o_ref[0] = jnp.where(valid, acc, NEG_INF)


def _bias_table(rel_table, dist, lo, hi, name):
    r, c = dist.shape
    return pl.pallas_call(
        functools.partial(_bias_body, lo=lo, hi=hi),
        grid=(N_HEADS,),
        in_specs=[pl.BlockSpec(memory_space=pltpu.SMEM),
                  pl.BlockSpec((r, c), lambda h: (0, 0))],
        out_specs=pl.BlockSpec((1, r, c), lambda h: (h, 0, 0)),
        out_shape=jax.ShapeDtypeStruct((N_HEADS, r, c), F32),
        compiler_params=_params(("arbitrary",)),
        name=name,
    )(rel_table, jnp.asarray(dist, jnp.int32))


BIG = 1 << 30


def _dist_tables():
    t128 = np.arange(QT)[None, :]
    tS = np.arange(SEQ)[None, :]
    t8 = PAST_LEN + np.arange(DEC_SEQ)[:, None]
    d = {}
    d["p_cmp"] = (tS - (np.arange(SEQ // CMP_STRIDE)[:, None] * CMP_STRIDE + CMP_BLOCK - 1), 0, BIG)
    d["p_win"] = (t128 + A_WINDOW - np.arange(A_WINDOW + QT)[:, None], 0, A_WINDOW)
    d["p_sel"] = (np.concatenate([u * QT + t128 - np.arange(QT)[:, None] for u in range(3)], axis=0), 0, BIG)
    d["p_swa"] = (t128 + B_WINDOW - np.arange(B_WINDOW + QT)[:, None], 0, B_WINDOW)
    n_cmp_s = PAST_LEN // CMP_STRIDE
    d["s_cmp"] = (t8 - (np.arange(n_cmp_s)[None, :] * CMP_STRIDE + CMP_BLOCK - 1), 0, BIG)
    d["s_win"] = (t8 - (PAST_LEN - A_WINDOW + np.arange(A_WINDOW + 16)[None, :]), 0, A_WINDOW)
    d["s_last"] = (t8 - (PAST_LEN - PAGE_SIZE + np.arange(PAGE_SIZE)[None, :]), 0, BIG)
    d["s_new"] = (t8 - (PAST_LEN + np.arange(16)[None, :]), 0, BIG)
    d["s_swa"] = (t8 - (PAST_LEN - B_WINDOW + np.arange(B_WINDOW + 16)[None, :]), 0, B_WINDOW)
    return d


CMP_PAIRS = CMP_BLOCK // 2


def _compress_body(pid_ref, pages_ref, wpair_ref, w2_ref, pe_ref, w1_ref, b1_ref, o_ref, buf, sem, *, pc, nch, npg):
    b = pl.program_id(0)
    c = pl.program_id(1)
    step = b * nch + c
    nsteps = pl.num_programs(0) * nch
    slot = step % 2
    m = pc * (PAGE_SIZE // CMP_STRIDE)
    n_chunks = A_KV_WIDTH // LANES

    def copies(bb, cc, sl):
        out = []
        for i in range(pc):
            pg = pid_ref[bb, cc * pc + i]
            for lc in range(n_chunks):
                out.append(pltpu.make_async_copy(pages_ref.at[pg, :, pl.ds(lc * LANES, LANES)],
                                                 buf.at[sl, lc, pl.ds(i * PAGE_SIZE, PAGE_SIZE), :], sem.at[sl]))
        pg = pid_ref[bb, jnp.minimum(cc * pc + pc, npg - 1)]
        for lc in range(n_chunks):
            out.append(pltpu.make_async_copy(pages_ref.at[pg, pl.ds(0, CMP_STRIDE), pl.ds(lc * LANES, LANES)],
                                             buf.at[sl, lc, pl.ds(pc * PAGE_SIZE, CMP_STRIDE), :], sem.at[sl]))
        return out

    @pl.when(step == 0)
    def _():
        for cp in copies(b, c, slot):
            cp.start()

    @pl.when(step + 1 < nsteps)
    def _():
        wrap = c + 1 == nch
        nb = jnp.where(wrap, b + 1, b)
        nc = jnp.where(wrap, 0, c + 1)
        for cp in copies(nb, nc, 1 - slot):
            cp.start()

    for cp in copies(b, c, slot):
        cp.wait()

    for kv in range(2):
        pe_term = jnp.dot(pe_ref[kv], w1_ref[kv], preferred_element_type=F32)[0:1] + b1_ref[kv]
        pe2 = jnp.concatenate([pe_term, pe_term], axis=1)
        acc = jnp.zeros((2 * m, 2 * CMP_HIDDEN), F32)
        for q in range(CMP_PAIRS):
            rows = []
            for j in range(2):
                a0 = buf[slot, 2 * kv + j, pl.ds(2 * q, m, stride=CMP_STRIDE), :]
                a1 = buf[slot, 2 * kv + j, pl.ds(2 * q + 1, m, stride=CMP_STRIDE), :]
                rows.append(jnp.concatenate([a0, a1], axis=1))
            lhs = jnp.concatenate(rows, axis=0).astype(BF16)
            acc = acc + jnp.dot(lhs, wpair_ref[kv, q], preferred_element_type=F32)
        hid = _silu(acc + pe2)
        out = jnp.dot(hid.astype(BF16), w2_ref[kv], preferred_element_type=F32)
        o_ref[0, kv, 0] = out[:m].astype(o_ref.dtype)
        o_ref[0, kv, 1] = out[m:].astype(o_ref.dtype)


def _compress(page_ids, pages, wpair, w2bd, pe8, w1f, b1, *, pc, name):
    nb, npg = page_ids.shape
    nch = npg // pc
    m = pc * (PAGE_SIZE // CMP_STRIDE)
    const = lambda nd: (lambda b, c, pid: (0,) * nd)
    grid_spec = pltpu.PrefetchScalarGridSpec(
        num_scalar_prefetch=1,
        grid=(nb, nch),
        in_specs=[pl.BlockSpec(memory_space=pl.ANY),
                  pl.BlockSpec(wpair.shape, const(4)),
                  pl.BlockSpec(w2bd.shape, const(3)),
                  pl.BlockSpec(pe8.shape, const(3)),
                  pl.BlockSpec(w1f.shape, const(3)),
                  pl.BlockSpec(b1.shape, const(3))],
        out_specs=pl.BlockSpec((1, 2, 2, m, LANES), lambda b, c, pid: (b, 0, 0, c, 0)),
        scratch_shapes=[pltpu.VMEM((2, A_KV_WIDTH // LANES, pc * PAGE_SIZE + CMP_STRIDE, LANES), F32),
                        pltpu.SemaphoreType.DMA((2,))],
    )
    return pl.pallas_call(
        functools.partial(_compress_body, pc=pc, nch=nch, npg=npg),
        grid_spec=grid_spec,
        out_shape=jax.ShapeDtypeStruct((nb, 2, 2, npg * (PAGE_SIZE // CMP_STRIDE), LANES), BF16),
        compiler_params=_params(("arbitrary", "arbitrary")),
        name=name,
    )(page_ids, pages, wpair, w2bd, pe8, w1f, b1)


def _select_mask(slc, tpos, blk, n_sel, top_n):
    avail = (blk * SEL_BLOCK <= tpos) & (blk < n_sel)
    lag = tpos // SEL_BLOCK - blk
    forced = (blk == 0) | ((lag >= 0) & (lag < N_LOCAL))
    slc = jnp.where(forced, FORCE, slc)
    slc = jnp.where(avail, slc, -FORCE)

    taken = jnp.zeros(slc.shape, F32)
    for _ in range(top_n):
        mx = jnp.max(slc, axis=1, keepdims=True)
        first = jnp.min(jnp.where(slc == mx, blk, BIG), axis=1, keepdims=True)
        hit = blk == first
        taken = jnp.where(hit, 1.0, taken)
        slc = jnp.where(hit, -jnp.inf, slc)
    return jnp.where(avail, taken, 0.0)


def _overlap_matrix(n_cmp, n_sel, rows, cols):
    msel = np.zeros((rows, cols), np.float32)
    per = SEL_BLOCK // CMP_STRIDE
    for j in range(n_sel):
        for mm in range(per):
            for r in range(CMP_BLOCK // CMP_STRIDE):
                n = per * j + mm - r
                if 0 <= n < n_cmp:
                    msel[n, j] += 1.0
    return msel


def _select_mask_t(slc, tpos, blk, n_sel, top_n):
    avail = blk * SEL_BLOCK <= tpos
    lag = tpos // SEL_BLOCK - blk
    forced = (blk == 0) | ((lag >= 0) & (lag < N_LOCAL))
    slc = jnp.where(forced, FORCE, slc)
    slc = jnp.where(avail, slc, -FORCE)
    cnt = jnp.zeros(slc.shape, F32)
    for jp in range(n_sel):
        row = slc[jp:jp + 1, :]
        ahead = (row > slc) | ((row == slc) & (blk > jp))
        cnt = cnt + jnp.where(ahead, 1.0, 0.0)
    return jnp.where(avail & (cnt < top_n), 1.0, 0.0)


M_FLOOR = 0.5 * NEG_INF


def _online_update(st, v, m_scr, l_scr, acc_scr):
    m_old = m_scr[...]
    m_new = jnp.maximum(m_old, jnp.max(st, axis=0, keepdims=True))
    a = jnp.exp(m_old - m_new)
    et = jnp.exp(st - m_new)
    l_scr[...] = a * l_scr[...] + jnp.sum(et, axis=0, keepdims=True)
    acc_scr[...] = a * acc_scr[...] + _tn_dot(v, et.astype(BF16))
    m_scr[...] = m_new


def _nsa_prompt_body(qt_ref, kc_ref, vc_ref, ks_ref, vs_ref, kw_ref, vw_ref, bc_ref, bw_ref, bt_ref, gl_ref,
                     mselt_ref, o_ref, sel_scr, m_scr, l_scr, acc_scr):
    i = pl.program_id(2)
    grp = A_GROUP
    n_sel = SEQ // SEL_BLOCK
    qt = qt_ref[0, 0, 0]

    bias_c = jnp.concatenate([bc_ref[0, g] for g in range(grp)], axis=1)
    p = _softmax_cols(jnp.dot(kc_ref[0, 0], qt, preferred_element_type=F32) + bias_c)
    oc = _tn_dot(vc_ref[0, 0], p.astype(BF16))

    score = p[:, 0:QT]
    for g in range(1, grp):
        score = score + p[:, g * QT:(g + 1) * QT]
    slc = jnp.dot(mselt_ref[...], score, precision=HIGHEST, preferred_element_type=F32)[0:n_sel]
    tpos = i * QT + lax.broadcasted_iota(jnp.int32, (n_sel, QT), 1)
    blk = lax.broadcasted_iota(jnp.int32, (n_sel, QT), 0)
    sel_scr[...] = _select_mask_t(slc, tpos, blk, n_sel, N_SELECT)

    m_scr[...] = jnp.full(m_scr.shape, M_FLOOR, F32)
    l_scr[...] = jnp.zeros(l_scr.shape, F32)
    acc_scr[...] = jnp.zeros(acc_scr.shape, F32)
    per_tile = QT // SEL_BLOCK

    def sel_tiles(jp, carry):
        pieces = []
        for w in range(2):
            jj = 2 * jp + w
            start = pl.multiple_of(jj * QT, QT)
            u = jnp.clip(i - jj, 0, 2)
            st = jnp.dot(ks_ref[0, 0, pl.ds(start, QT), :], qt, preferred_element_type=F32) + bt_ref[0, u]
            for hb in range(per_tile):
                r = sel_scr[pl.ds(per_tile * jj + hb, 1), :]
                off = (jnp.concatenate([r] * grp, axis=1) - 1.0) * (-NEG_INF)
                pieces.append(st[hb * SEL_BLOCK:(hb + 1) * SEL_BLOCK] + off)
        start2 = pl.multiple_of(jp * (2 * QT), 2 * QT)
        _online_update(jnp.concatenate(pieces, axis=0), vs_ref[0, 0, pl.ds(start2, 2 * QT), :], m_scr, l_scr, acc_scr)
        return carry

    lax.fori_loop(0, i // 2 + 1, sel_tiles, 0)
    osel = acc_scr[...] / jnp.maximum(l_scr[...], TINY)

    n_t = A_WINDOW // QT + 1
    parts = []
    starts = []
    for u in range(n_t):
        j = i - (n_t - 1) + u
        start = pl.multiple_of(jnp.maximum(j, 0) * QT, QT)
        starts.append(start)
        su = jnp.dot(kw_ref[0, 0, pl.ds(start, QT), :], qt, preferred_element_type=F32) + bw_ref[0, u * QT:(u + 1) * QT, :]
        parts.append(jnp.where(j >= 0, su, NEG_INF))
    pw = _softmax_cols(jnp.concatenate(parts, axis=0)).astype(BF16)
    ow = jnp.zeros((HEAD_DIM, grp * QT), F32)
    for u in range(n_t):
        ow = ow + _tn_dot(vw_ref[0, 0, pl.ds(starts[u], QT), :], pw[u * QT:(u + 1) * QT])

    gates = 1.0 / (1.0 + jnp.exp(-gl_ref[0, 0, 0]))
    o_ref[0, 0, 0] = (gates[0:1] * oc + gates[1:2] * osel + gates[2:3] * ow).astype(o_ref.dtype)


def _nsa_prompt(qt, kc, vc, ks, vs, kw, vw, bias_cmp, bias_win, bias_sel, gate_t):
    nq = SEQ // QT
    grp = A_GROUP
    cols = grp * QT
    n_cmp = SEQ // CMP_STRIDE
    mselt = jnp.asarray(_overlap_matrix(n_cmp - 1, SEQ // SEL_BLOCK, n_cmp, LANES).T.copy())
    kv_spec = lambda n: pl.BlockSpec((1, 1, n, HEAD_DIM), lambda b, k, i: (b, k, 0, 0))
    tile_spec = lambda r: pl.BlockSpec((1, 1, 1, r, cols), lambda b, k, i: (b, k, i, 0, 0))
    n_win = A_WINDOW + QT
    return pl.pallas_call(
        _nsa_prompt_body,
        grid=(BATCH, A_KV_HEADS, nq),
        in_specs=[tile_spec(HEAD_DIM),
                  kv_spec(n_cmp), kv_spec(n_cmp), kv_spec(SEQ), kv_spec(SEQ), kv_spec(SEQ), kv_spec(SEQ),
                  pl.BlockSpec((1, grp, n_cmp, QT), lambda b, k, i: (k, 0, 0, i)),
                  pl.BlockSpec((1, n_win, cols), lambda b, k, i: (k, 0, 0)),
                  pl.BlockSpec((1, 3, QT, cols), lambda b, k, i: (k, 0, 0, 0)),
                  tile_spec(3),
                  pl.BlockSpec((LANES, n_cmp), lambda b, k, i: (0, 0))],
        out_specs=tile_spec(HEAD_DIM),
        out_shape=jax.ShapeDtypeStruct((BATCH, A_KV_HEADS, nq, HEAD_DIM, cols), BF16),
        scratch_shapes=[pltpu.VMEM((SEQ // SEL_BLOCK, QT), F32),
                        pltpu.VMEM((1, cols), F32), pltpu.VMEM((1, cols), F32),
                        pltpu.VMEM((HEAD_DIM, cols), F32)],
        compiler_params=_params(("arbitrary", "arbitrary", "arbitrary")),
        name="nsa_prompt",
    )(qt, kc, vc, ks, vs, kw, vw, bias_cmp, bias_win, bias_sel, gate_t, mselt)


SEL_PAGES = 16
S_COLS = A_KV_HEADS * A_GROUP * DEC_SEQ
H_COLS = A_GROUP * DEC_SEQ
S_NSEL = -(-(PAST_LEN + DEC_SEQ) // SEL_BLOCK)
S_NSEL_PAD = 384


def _softmax_cols(s):
    mx = jnp.maximum(jnp.max(s, axis=0, keepdims=True), M_FLOOR)
    e = jnp.exp(s - mx)
    return e / jnp.maximum(jnp.sum(e, axis=0, keepdims=True), TINY)


def _nsa_sample_body(pid_ref, qt_ref, kc_ref, vc_ref, kw_ref, vw_ref, kn_ref, vn_ref, bc_ref, bw_ref, bl_ref, bn_ref,
                     cf_ref, gl_ref, gsum_ref, msel_ref, gexp_ref, eexp_ref, pages_ref, o_ref,
                     buf, sem, sel_scr, oc_scr, ow_scr, m_scr, l_scr, acc_scr, *, npg):
    b = pl.program_id(0)
    c = pl.program_id(1)
    nch = npg // SEL_PAGES
    step = b * nch + c
    nsteps = pl.num_programs(0) * nch
    slot = step % 2

    def copies(bb, cc, sl):
        return [pltpu.make_async_copy(pages_ref.at[pid_ref[bb, cc * SEL_PAGES + i]], buf.at[sl, i], sem.at[sl])
                for i in range(SEL_PAGES)]

    @pl.when(step == 0)
    def _():
        for cp in copies(b, c, slot):
            cp.start()

    @pl.when(step + 1 < nsteps)
    def _():
        wrap = c + 1 == nch
        for cp in copies(jnp.where(wrap, b + 1, b), jnp.where(wrap, 0, c + 1), 1 - slot):
            cp.start()

    qt = qt_ref[0]

    @pl.when(c == 0)
    def _():
        p = _softmax_cols(jnp.dot(kc_ref[0], qt, preferred_element_type=F32) + bc_ref[...])
        oc_scr[...] = _tn_dot(vc_ref[0], p.astype(BF16))
        score = lax.dot_general(gsum_ref[...], p, (((1,), (1,)), ((), ())), precision=HIGHEST,
                                preferred_element_type=F32)
        slc = jnp.dot(score, msel_ref[...], precision=HIGHEST, preferred_element_type=F32)
        r = lax.broadcasted_iota(jnp.int32, slc.shape, 0)
        tpos = PAST_LEN + r % DEC_SEQ
        blk = lax.broadcasted_iota(jnp.int32, slc.shape, 1)
        sel = _select_mask(slc, tpos, blk, S_NSEL, N_SELECT).astype(BF16)
        sel_scr[...] = _tn_dot(sel, gexp_ref[...])
        pw = _softmax_cols(jnp.dot(kw_ref[0], qt, preferred_element_type=F32) + bw_ref[...])
        ow_scr[...] = _tn_dot(vw_ref[0], pw.astype(BF16))
        m_scr[...] = jnp.full(m_scr.shape, M_FLOOR, F32)
        l_scr[...] = jnp.zeros(l_scr.shape, F32)
        acc_scr[...] = jnp.zeros(acc_scr.shape, F32)

    for cp in copies(b, c, slot):
        cp.wait()

    nk = SEL_PAGES * PAGE_SIZE
    blocks = nk // SEL_BLOCK
    selc = sel_scr[pl.ds(pl.multiple_of(c * blocks, blocks), blocks), :].astype(BF16)
    keymask = jnp.dot(eexp_ref[...], selc, preferred_element_type=F32)
    far = jnp.broadcast_to(cf_ref[...], (PAGE_SIZE, S_COLS))
    last = jnp.where(c == nch - 1, bl_ref[...], far)
    for h in range(A_KV_HEADS):
        cs = slice(h * H_COLS, (h + 1) * H_COLS)
        ds_ = slice(h * HEAD_DIM, (h + 1) * HEAD_DIM)
        kt = jnp.concatenate([buf[slot, i, h] for i in range(SEL_PAGES)], axis=1).astype(BF16)
        vt = jnp.concatenate([buf[slot, i, A_KV_HEADS + h] for i in range(SEL_PAGES)], axis=1).astype(BF16)
        st = _tn_dot(kt, qt[ds_, cs])
        st = jnp.concatenate([st[:nk - PAGE_SIZE] + cf_ref[:, cs], st[nk - PAGE_SIZE:] + last[:, cs]], axis=0)
        st = jnp.where(keymask[:, cs] > 0.5, st, NEG_INF)
        m_old = m_scr[:, cs]
        m_new = jnp.maximum(m_old, jnp.max(st, axis=0, keepdims=True))
        a = jnp.exp(m_old - m_new)
        et = jnp.exp(st - m_new)
        l_scr[:, cs] = a * l_scr[:, cs] + jnp.sum(et, axis=0, keepdims=True)
        acc_scr[ds_, cs] = a * acc_scr[ds_, cs] + jnp.dot(vt, et.astype(BF16), preferred_element_type=F32)
        m_scr[:, cs] = m_new

    @pl.when(c == nch - 1)
    def _():
        sn = jnp.dot(kn_ref[0], qt, preferred_element_type=F32) + bn_ref[...]
        new_blk = PAST_LEN // SEL_BLOCK
        sn = jnp.where(sel_scr[new_blk:new_blk + 1, :] > 0.5, sn, NEG_INF)
        _online_update(sn, vn_ref[0], m_scr, l_scr, acc_scr)
        osel = acc_scr[...] / jnp.maximum(l_scr[...], TINY)
        gates = 1.0 / (1.0 + jnp.exp(-gl_ref[0]))
        o_ref[0] = gates[0:1] * oc_scr[...] + gates[1:2] * osel + gates[2:3] * ow_scr[...]


def _nsa_sample(page_table, qt, kc, vc, kwin, vwin, knew, vnew, b_cmp, b_win, b_last, b_new, c_far, gate_t, pages):
    nch = N_PAGES // SEL_PAGES
    n_cmp = PAST_LEN // CMP_STRIDE
    kwid = A_KV_HEADS * HEAD_DIM
    gsum = np.zeros((A_KV_HEADS * DEC_SEQ, S_COLS), np.float32)
    for k in range(A_KV_HEADS):
        for g in range(A_GROUP):
            for t in range(DEC_SEQ):
                gsum[k * DEC_SEQ + t, (k * A_GROUP + g) * DEC_SEQ + t] = 1.0
    msel = _overlap_matrix(n_cmp - 1, S_NSEL, n_cmp, S_NSEL_PAD)
    nk = SEL_PAGES * PAGE_SIZE
    eexp = np.zeros((nk, nk // SEL_BLOCK), np.float32)
    eexp[np.arange(nk), np.arange(nk) // SEL_BLOCK] = 1.0
    per_b = lambda shape: pl.BlockSpec((1,) + shape, lambda b, c, pid: (b, 0, 0))
    const2 = lambda shape: pl.BlockSpec(shape, lambda b, c, pid: (0, 0))
    nwin = kwin.shape[1]
    grid_spec = pltpu.PrefetchScalarGridSpec(
        num_scalar_prefetch=1,
        grid=(DEC_BATCH, nch),
        in_specs=[per_b((kwid, S_COLS)),
                  per_b((n_cmp, kwid)), per_b((n_cmp, kwid)),
                  per_b((nwin, kwid)), per_b((nwin, kwid)),
                  per_b((16, kwid)), per_b((16, kwid)),
                  const2((n_cmp, S_COLS)), const2((nwin, S_COLS)), const2((PAGE_SIZE, S_COLS)), const2((16, S_COLS)),
                  const2((1, S_COLS)), per_b((3, S_COLS)),
                  const2(gsum.shape), const2(msel.shape), const2(gsum.shape), const2(eexp.shape),
                  pl.BlockSpec(memory_space=pl.ANY)],
        out_specs=pl.BlockSpec((1, kwid, S_COLS), lambda b, c, pid: (b, 0, 0)),
        scratch_shapes=[pltpu.VMEM((2, SEL_PAGES, 2 * A_KV_HEADS, HEAD_DIM, PAGE_SIZE), F32),
                        pltpu.SemaphoreType.DMA((2,)),
                        pltpu.VMEM((S_NSEL_PAD, S_COLS), F32),
                        pltpu.VMEM((kwid, S_COLS), F32), pltpu.VMEM((kwid, S_COLS), F32),
                        pltpu.VMEM((1, S_COLS), F32), pltpu.VMEM((1, S_COLS), F32),
                        pltpu.VMEM((kwid, S_COLS), F32)],
    )
    return pl.pallas_call(
        functools.partial(_nsa_sample_body, npg=N_PAGES),
        grid_spec=grid_spec,
        out_shape=jax.ShapeDtypeStruct((DEC_BATCH, kwid, S_COLS), F32),
        compiler_params=_params(("arbitrary", "arbitrary")),
        name="nsa_sample",
    )(page_table, qt, kc, vc, kwin, vwin, knew, vnew, b_cmp, b_win, b_last, b_new, c_far, gate_t,
      jnp.asarray(gsum), jnp.asarray(msel), jnp.asarray(gsum, BF16), jnp.asarray(eexp, BF16), pages)


def _swa_prompt_body(qt_ref, k_ref, v_ref, bias_ref, sink_ref, o_ref):
    i = pl.program_id(2)
    qt = qt_ref[0, 0, 0]
    parts = []
    starts = []
    for u in range(2):
        j = i - 1 + u
        start = pl.multiple_of(jnp.maximum(j, 0) * QT, QT)
        starts.append(start)
        su = jnp.dot(k_ref[0, 0, pl.ds(start, QT), :], qt, preferred_element_type=F32) + bias_ref[0, u * QT:(u + 1) * QT, :]
        parts.append(jnp.where(j >= 0, su, NEG_INF))
    s = jnp.concatenate(parts, axis=0)
    sink = sink_ref[0]
    mx = jnp.maximum(jnp.max(s, axis=0, keepdims=True), sink)
    e = jnp.exp(s - mx)
    den = jnp.sum(e, axis=0, keepdims=True) + jnp.exp(sink - mx)
    p = (e / jnp.maximum(den, TINY)).astype(BF16)
    o = jnp.zeros((HEAD_DIM, qt.shape[1]), F32)
    for u in range(2):
        o = o + _tn_dot(v_ref[0, 0, pl.ds(starts[u], QT), :], p[u * QT:(u + 1) * QT])
    o_ref[0, 0, 0] = o.astype(o_ref.dtype)


def _swa_prompt(qt, k, v, bias, sink_cols):
    nq = SEQ // QT
    cols = B_GROUP * QT
    kv_spec = pl.BlockSpec((1, 1, SEQ, HEAD_DIM), lambda b, h, i: (b, h, 0, 0))
    tile = pl.BlockSpec((1, 1, 1, HEAD_DIM, cols), lambda b, h, i: (b, h, i, 0, 0))
    return pl.pallas_call(
        _swa_prompt_body,
        grid=(BATCH, B_KV_HEADS, nq),
        in_specs=[tile, kv_spec, kv_spec,
                  pl.BlockSpec((1, 2 * QT, cols), lambda b, h, i: (h, 0, 0)),
                  pl.BlockSpec((1, 1, cols), lambda b, h, i: (h, 0, 0))],
        out_specs=tile,
        out_shape=jax.ShapeDtypeStruct((BATCH, B_KV_HEADS, nq, HEAD_DIM, cols), BF16),
        compiler_params=_params(("arbitrary", "arbitrary", "arbitrary")),
        name="swa_prompt",
    )(qt, k, v, bias, sink_cols)


def _swa_sample_body(qt_ref, k_ref, v_ref, bias_ref, sink_ref, o_ref):
    s = jnp.dot(k_ref[0], qt_ref[0], preferred_element_type=F32) + bias_ref[...]
    sink = sink_ref[...]
    mx = jnp.maximum(jnp.max(s, axis=0, keepdims=True), sink)
    e = jnp.exp(s - mx)
    den = jnp.sum(e, axis=0, keepdims=True) + jnp.exp(sink - mx)
    p = e / jnp.maximum(den, TINY)
    o_ref[0] = _tn_dot(v_ref[0], p.astype(BF16))


def _swa_sample(qt, k, v, bias_t, sink_row):
    nk = k.shape[1]
    w = B_KV_HEADS * HEAD_DIM
    per_b = lambda shape: pl.BlockSpec((1,) + shape, lambda b: (b, 0, 0))
    return pl.pallas_call(
        _swa_sample_body,
        grid=(DEC_BATCH,),
        in_specs=[per_b((w, S_COLS)), per_b((nk, w)), per_b((nk, w)),
                  pl.BlockSpec((nk, S_COLS), lambda b: (0, 0)), pl.BlockSpec((1, S_COLS), lambda b: (0, 0))],
        out_specs=per_b((w, S_COLS)),
        out_shape=jax.ShapeDtypeStruct((DEC_BATCH, w, S_COLS), F32),
        compiler_params=_params(("arbitrary",)),
        name="swa_sample",
    )(qt, k, v, bias_t, sink_row)


MOE_TM = 512
MOE_TILES = -(-(TOP_K * MT + N_EXPERTS * (MOE_TM - 1)) // MOE_TM)
MOE_ROWS = MOE_TILES * MOE_TM
MOE_TN_UP = 512
MOE_TN_DOWN = 1024
MOE_SUB = 512
MOE_KSPLIT = 4
MOE_GATHER_ROWS = 256


def _router_body(x_ref, w_ref, b_ref, o_ref):
    logits = jnp.dot(x_ref[...], w_ref[...], precision=HIGHEST, preferred_element_type=F32) + b_ref[...]
    lane = lax.broadcasted_iota(jnp.int32, logits.shape, 1)
    logits = jnp.where(lane < N_EXPERTS, logits, -jnp.inf)
    v1 = jnp.max(logits, axis=1, keepdims=True)
    i1 = jnp.min(jnp.where(logits == v1, lane, LANES), axis=1, keepdims=True)
    rest = jnp.where(lane == i1, -jnp.inf, logits)
    v2 = jnp.max(rest, axis=1, keepdims=True)
    i2 = jnp.min(jnp.where(rest == v2, lane, LANES), axis=1, keepdims=True)
    e2 = jnp.exp(v2 - v1)
    den = 1.0 + e2
    out = jnp.where(lane == 0, i1.astype(F32),
                    jnp.where(lane == 1, i2.astype(F32),
                              jnp.where(lane == 2, 1.0 / den, jnp.where(lane == 3, e2 / den, 0.0))))
    o_ref[...] = out


def _router(x, w, b, *, tm):
    m, d = x.shape
    wp = jnp.zeros((d, LANES), F32).at[:, :N_EXPERTS].set(w)
    bp = jnp.zeros((1, LANES), F32).at[0, :N_EXPERTS].set(b)
    return pl.pallas_call(
        _router_body,
        grid=(m // tm,),
        in_specs=[pl.BlockSpec((tm, d), lambda i: (i, 0)), pl.BlockSpec((d, LANES), lambda i: (0, 0)),
                  pl.BlockSpec((1, LANES), lambda i: (0, 0))],
        out_specs=pl.BlockSpec((tm, LANES), lambda i: (i, 0)),
        out_shape=jax.ShapeDtypeStruct((m, LANES), F32),
        compiler_params=_params(("arbitrary",)),
        name="moe_router",
    )(x, wp, bp)


def _gather_body(idx_ref, src_ref, o_ref, *scratch, rows, staged):
    if staged:
        stage, sem = scratch
    else:
        (sem,) = scratch
        stage = o_ref
    base = pl.program_id(0) * rows

    def row_copy(r, src_row):
        return pltpu.make_async_copy(src_ref.at[pl.ds(src_row, 1)], stage.at[pl.ds(r, 1)], sem)

    def issue(r, carry):
        row_copy(r, idx_ref[base + r]).start()
        return carry

    def drain(r, carry):
        row_copy(r, 0).wait()
        return carry

    lax.fori_loop(0, rows, issue, 0)
    lax.fori_loop(0, rows, drain, 0)
    if staged:
        o_ref[...] = stage[...].astype(o_ref.dtype)


def _gather_rows(src, idx, *, rows, out_dtype, name):
    n = idx.shape[0]
    w = src.shape[1]
    staged = out_dtype != src.dtype
    scratch = ([pltpu.VMEM((rows, w), src.dtype)] if staged else []) + [pltpu.SemaphoreType.DMA(())]
    grid_spec = pltpu.PrefetchScalarGridSpec(
        num_scalar_prefetch=1,
        grid=(n // rows,),
        in_specs=[pl.BlockSpec(memory_space=pl.ANY)],
        out_specs=pl.BlockSpec((rows, w), lambda i, idx_ref: (i, 0)),
        scratch_shapes=scratch,
    )
    return pl.pallas_call(
        functools.partial(_gather_body, rows=rows, staged=staged),
        grid_spec=grid_spec,
        out_shape=jax.ShapeDtypeStruct((n, w), out_dtype),
        compiler_params=_params(("arbitrary",)),
        name=name,
    )(idx, src)


def _moe_up_body(te_ref, nv_ref, nxt_ref, x_ref, w_hbm, o_ref, wbuf, sem, cnt_ref, wg_b, wu_b, *, nb):
    j = pl.program_id(0)
    i = pl.program_id(1)
    e = te_ref[i]
    prev = te_ref[jnp.maximum(i - 1, 0)]

    def group_copies(ee, jj, sl):
        return [pltpu.make_async_copy(w_hbm.at[ee, :, pl.ds(jj * MOE_TN_UP, MOE_TN_UP)], wbuf.at[sl, 0], sem.at[sl]),
                pltpu.make_async_copy(w_hbm.at[ee, :, pl.ds((jj + nb) * MOE_TN_UP, MOE_TN_UP)], wbuf.at[sl, 1], sem.at[sl])]

    @pl.when((j == 0) & (i == 0))
    def _():
        cnt_ref[0] = 0
        for cp in group_copies(e, j, 0):
            cp.start()

    @pl.when((i == 0) | (e != prev))
    def _():
        slot = cnt_ref[0] % 2
        for cp in group_copies(e, j, slot):
            cp.wait()
        nxt = nxt_ref[i]
        wrap = nxt < 0
        ne = jnp.where(wrap, te_ref[0], nxt)
        nj = jnp.where(wrap, j + 1, j)

        @pl.when(nj < nb)
        def _():
            for cp in group_copies(ne, nj, 1 - slot):
                cp.start()

        wg_b[...] = wbuf[slot, 0].astype(BF16)
        wu_b[...] = wbuf[slot, 1].astype(BF16)
        cnt_ref[0] = cnt_ref[0] + 1

    @pl.when(i < nv_ref[0])
    def _():
        x = x_ref[...]
        g = jnp.dot(x, wg_b[...], preferred_element_type=F32)
        u = jnp.dot(x, wu_b[...], preferred_element_type=F32)
        o_ref[...] = (_silu(g) * u).astype(o_ref.dtype)

    @pl.when(i >= nv_ref[0])
    def _():
        o_ref[...] = jnp.zeros(o_ref.shape, o_ref.dtype)


def _moe_up(tile_expert, n_valid, next_expert, xs, w_in):
    d = xs.shape[1]
    f = w_in.shape[2] // 2
    nb = f // MOE_TN_UP
    grid_spec = pltpu.PrefetchScalarGridSpec(
        num_scalar_prefetch=3,
        grid=(nb, MOE_TILES),
        in_specs=[pl.BlockSpec((MOE_TM, d), lambda j, i, te, nv, nx: (i, 0)),
                  pl.BlockSpec(memory_space=pl.ANY)],
        out_specs=pl.BlockSpec((MOE_TM, MOE_TN_UP), lambda j, i, te, nv, nx: (i, j)),
        scratch_shapes=[pltpu.VMEM((2, 2, d, MOE_TN_UP), F32), pltpu.SemaphoreType.DMA((2,)),
                        pltpu.SMEM((1,), jnp.int32),
                        pltpu.VMEM((d, MOE_TN_UP), BF16), pltpu.VMEM((d, MOE_TN_UP), BF16)],
    )
    return pl.pallas_call(
        functools.partial(_moe_up_body, nb=nb),
        grid_spec=grid_spec,
        out_shape=jax.ShapeDtypeStruct((MOE_ROWS, f), BF16),
        compiler_params=_params(("arbitrary", "arbitrary")),
        name="moe_up",
    )(tile_expert, n_valid, next_expert, xs, w_in)


def _moe_down_body(te_ref, nv_ref, *refs, first):
    if first:
        x_ref, w_ref, o_ref, w_b = refs
        prev_ref = None
    else:
        x_ref, w_ref, prev_ref, o_ref, w_b = refs
    i = pl.program_id(1)
    per = MOE_TM // MOE_SUB
    e = te_ref[i // per]
    prev = te_ref[jnp.maximum(i - 1, 0) // per]

    @pl.when((i == 0) | (e != prev))
    def _():
        w_b[...] = w_ref[0].astype(BF16)

    @pl.when(i // per < nv_ref[0])
    def _():
        acc = jnp.dot(x_ref[...], w_b[...], preferred_element_type=F32)
        o_ref[...] = acc if first else acc + prev_ref[...]

    @pl.when(i // per >= nv_ref[0])
    def _():
        o_ref[...] = jnp.zeros(o_ref.shape, o_ref.dtype)


def _moe_down(tile_expert, n_valid, h, w_out, prev, *, part):
    f = h.shape[1]
    d = w_out.shape[2]
    tk = f // MOE_KSPLIT
    per = MOE_TM // MOE_SUB
    first = prev is None
    in_specs = [pl.BlockSpec((MOE_SUB, tk), lambda j, i, te, nv: (i, part)),
                pl.BlockSpec((1, tk, MOE_TN_DOWN), lambda j, i, te, nv: (te[i // per], part, j))]
    args = [h, w_out]
    aliases = {}
    if not first:
        in_specs.append(pl.BlockSpec((MOE_SUB, MOE_TN_DOWN), lambda j, i, te, nv: (i, j)))
        args.append(prev)
        aliases = {4: 0}
    grid_spec = pltpu.PrefetchScalarGridSpec(
        num_scalar_prefetch=2,
        grid=(d // MOE_TN_DOWN, MOE_TILES * per),
        in_specs=in_specs,
        out_specs=pl.BlockSpec((MOE_SUB, MOE_TN_DOWN), lambda j, i, te, nv: (i, j)),
        scratch_shapes=[pltpu.VMEM((tk, MOE_TN_DOWN), BF16)],
    )
    return pl.pallas_call(
        functools.partial(_moe_down_body, first=first),
        grid_spec=grid_spec,
        out_shape=jax.ShapeDtypeStruct((MOE_ROWS, d), F32),
        input_output_aliases=aliases,
        compiler_params=_params(("arbitrary", "arbitrary")),
        name=f"moe_down_{part}",
    )(tile_expert, n_valid, *args)


def _route(top):
    idx = top[:, 0:TOP_K].astype(jnp.int32)
    gate = top[:, TOP_K:2 * TOP_K]
    flat_e = idx.reshape(-1)
    onehot = (flat_e[:, None] == jnp.arange(N_EXPERTS)[None, :]).astype(jnp.int32)
    csum = jnp.cumsum(onehot, axis=0)
    counts = csum[-1]
    rank = jnp.take_along_axis(csum, flat_e[:, None], axis=1)[:, 0] - 1
    tiles_per = (counts + MOE_TM - 1) // MOE_TM
    tile_end = jnp.cumsum(tiles_per)
    tile_start = tile_end - tiles_per
    pos = tile_start[flat_e] * MOE_TM + rank
    src_tok = jnp.zeros((MOE_ROWS,), jnp.int32).at[pos].set(jnp.arange(TOP_K * MT, dtype=jnp.int32) // TOP_K)
    n_valid = tile_end[-1]
    tile_ids = jnp.arange(MOE_TILES, dtype=jnp.int32)
    tile_expert = jnp.searchsorted(tile_end, jnp.minimum(tile_ids, n_valid - 1), side="right").astype(jnp.int32)
    tile_expert = jnp.minimum(tile_expert, N_EXPERTS - 1)
    after = tile_end[tile_expert]
    next_expert = jnp.where(after < n_valid, tile_expert[jnp.minimum(after, MOE_TILES - 1)], -1).astype(jnp.int32)
    return gate, pos.reshape(MT, TOP_K), src_tok, tile_expert, n_valid.reshape(1).astype(jnp.int32), next_expert


def _moe(x_f32, w_router, b_router, w_in, w_out):
    top = _router(x_f32, w_router, b_router, tm=264)
    gate, pos, src_tok, tile_expert, n_valid, next_expert = _route(top)
    xs = _gather_rows(x_f32, src_tok, rows=MOE_GATHER_ROWS, out_dtype=BF16, name="moe_dispatch")
    h = _moe_up(tile_expert, n_valid, next_expert, xs, w_in)
    y = None
    for part in range(MOE_KSPLIT):
        y = _moe_down(tile_expert, n_valid, h, w_out, y, part=part)
    back = jnp.concatenate([pos[:, 0], pos[:, 1]])
    yy = _gather_rows(y, back, rows=MOE_GATHER_ROWS, out_dtype=F32, name="moe_collect")
    return yy, gate


def _tile_q(q, kvh, grp):
    nq = SEQ // QT
    q6 = q.reshape(BATCH, nq, QT, kvh, grp, HEAD_DIM)
    return q6.transpose(0, 3, 1, 5, 4, 2).reshape(BATCH, kvh, nq, HEAD_DIM, grp * QT)


def _untile_o(ot, kvh, grp):
    nq = SEQ // QT
    o6 = ot.reshape(BATCH, kvh, nq, HEAD_DIM, grp, QT)
    return o6.transpose(0, 2, 5, 1, 4, 3).reshape(MP, kvh * grp * HEAD_DIM)


def _tile_cols(tab, kvh, grp):
    rows = tab.shape[1]
    return tab.reshape(kvh, grp, rows, QT).transpose(0, 2, 1, 3).reshape(kvh, rows, grp * QT)


def _block_q(q, kvh, grp):
    q5 = q.reshape(DEC_BATCH, DEC_SEQ, kvh, grp, HEAD_DIM)
    eye = jnp.eye(kvh, dtype=q.dtype)
    qt = jnp.einsum("btkgd,kl->bldkgt", q5, eye)
    return qt.reshape(DEC_BATCH, kvh * HEAD_DIM, kvh * grp * DEC_SEQ)


def _unblock_o(ot, kvh, grp):
    o6 = ot.reshape(DEC_BATCH, kvh, HEAD_DIM, kvh, grp, DEC_SEQ)
    o = jnp.einsum("bkdkgt->btkgd", o6)
    return o.reshape(MS, kvh * grp * HEAD_DIM)


def _cols_t(tab):
    return tab.transpose(2, 0, 1).reshape(tab.shape[2], N_HEADS * DEC_SEQ)


def kernel(x_prompt, x_sample, cache_cmp_kv, cache_sel_kv, state_win_kv, state_shared_kv, page_table, rel_table, ln_g, ln_b, a_w_in, a_cmp_pe, a_cmp_w1, a_cmp_b1, a_cmp_w2, a_w_out, b_w_kv, b_w_q, b_sinks, b_w_out, dense_w_in, dense_w_out, moe_router_w, moe_router_b, moe_w_in, moe_w_out):
    n_phys = cache_cmp_kv.shape[1]
    x0 = jnp.concatenate([x_prompt.reshape(MP, D_MODEL), x_sample.reshape(MS, D_MODEL)], axis=0)
    x0b = x0.astype(BF16)

    dist = _dist_tables()
    tabs = {k: _bias_table(rel_table, d, lo, hi, "bias_" + k) for k, (d, lo, hi) in dist.items()}
    c_far = jnp.repeat(rel_table[REL_BUCKETS - 1], DEC_SEQ).reshape(1, N_HEADS * DEC_SEQ)

    la = 0
    w_in = a_w_in[la]
    kvg_w = jnp.pad(w_in[:, Q_WIDTH:], ((0, 0), (0, 64))).astype(BF16)
    q = _matmul(x0b, w_in[:, :Q_WIDTH].astype(BF16), tm=TM, tn=1024, out_dtype=BF16, scale=SCALE, name="a_q_proj")
    hk = _matmul(x0b, kvg_w, tm=TM, tn=896, out_dtype=F32, name="a_kv_proj")
    kvw = A_KV_WIDTH
    kv_c, kv_s, kv_w = hk[:, 0:kvw], hk[:, kvw:2 * kvw], hk[:, 2 * kvw:3 * kvw]
    gate_logits = hk[:, 3 * kvw:3 * kvw + 3 * N_HEADS]

    w1 = a_cmp_w1[la]
    eye2 = jnp.eye(2, dtype=F32)
    wpair = jnp.einsum("cqpdh,kl->cqpkdlh", w1.reshape(2, CMP_PAIRS, 2, HEAD_DIM, CMP_HIDDEN), eye2)
    wpair = wpair.reshape(2, CMP_PAIRS, 4 * HEAD_DIM, 2 * CMP_HIDDEN).astype(BF16)
    w2bd = jnp.einsum("chd,kl->ckhld", a_cmp_w2[la], eye2).reshape(2, 2 * CMP_HIDDEN, 2 * HEAD_DIM).astype(BF16)
    pe8 = jnp.broadcast_to(a_cmp_pe[la].reshape(2, 1, CMP_BLOCK * HEAD_DIM), (2, 8, CMP_BLOCK * HEAD_DIM)).astype(BF16)
    w1f = w1.reshape(2, CMP_BLOCK * HEAD_DIM, CMP_HIDDEN).astype(BF16)
    b1 = a_cmp_b1[la].reshape(2, 1, CMP_HIDDEN)

    pages_per_seq = SEQ // PAGE_SIZE
    prompt_ids = jnp.arange(BATCH * pages_per_seq, dtype=jnp.int32).reshape(BATCH, pages_per_seq)
    ckv_p = _compress(prompt_ids, kv_c[:MP].reshape(BATCH * pages_per_seq, PAGE_SIZE, kvw), wpair, w2bd, pe8, w1f, b1,
                      pc=pages_per_seq, name="cmp_prompt")
    sample_ids = la * n_phys + page_table
    ckv_s = _compress(sample_ids, cache_cmp_kv.reshape(-1, PAGE_SIZE, kvw), wpair, w2bd, pe8, w1f, b1,
                      pc=32, name="cmp_sample")

    n_cmp_p = SEQ // CMP_STRIDE
    ckv_ph = ckv_p.reshape(BATCH, 2, 2, n_cmp_p, 2, HEAD_DIM).transpose(0, 1, 2, 4, 3, 5).reshape(BATCH, 2, A_KV_HEADS, n_cmp_p, HEAD_DIM)

    def kv_heads(kv):
        kv5 = kv.reshape(BATCH, SEQ, 2, A_KV_HEADS, HEAD_DIM).astype(BF16).transpose(2, 0, 3, 1, 4)
        return kv5[0], kv5[1]

    ks_p, vs_p = kv_heads(kv_s[:MP])
    kw_p, vw_p = kv_heads(kv_w[:MP])
    grp = A_GROUP
    nq = SEQ // QT
    bias_cmp = tabs["p_cmp"].reshape(A_KV_HEADS, grp, n_cmp_p, SEQ)
    bias_win = _tile_cols(tabs["p_win"], A_KV_HEADS, grp)
    bias_sel = _tile_cols(tabs["p_sel"], A_KV_HEADS, grp).reshape(A_KV_HEADS, 3, QT, grp * QT)
    gl_p = gate_logits[:MP].reshape(BATCH, nq, QT, 3, A_KV_HEADS, grp).transpose(0, 4, 1, 3, 5, 2).reshape(BATCH, A_KV_HEADS, nq, 3, grp * QT)
    ot_p = _nsa_prompt(_tile_q(q[:MP], A_KV_HEADS, grp), ckv_ph[:, 0], ckv_ph[:, 1], ks_p, vs_p, kw_p, vw_p,
                       bias_cmp, bias_win, bias_sel, gl_p)
    o_ap = _untile_o(ot_p, A_KV_HEADS, grp)

    q_s = q[MP:].reshape(DEC_BATCH, DEC_SEQ, Q_WIDTH)
    qt_a = _block_q(q_s, A_KV_HEADS, A_GROUP)
    n_cmp_s = PAST_LEN // CMP_STRIDE
    ckv_sa = ckv_s.reshape(DEC_BATCH, 2, 2, n_cmp_s, 2 * HEAD_DIM).transpose(0, 1, 3, 2, 4).reshape(DEC_BATCH, 2, n_cmp_s, 4 * HEAD_DIM)
    kv_w_new = kv_w[MP:].reshape(DEC_BATCH, DEC_SEQ, kvw)
    win_full = jnp.concatenate([state_win_kv[la].reshape(DEC_BATCH, A_WINDOW, kvw), kv_w_new], axis=1)
    win_pad = jnp.pad(win_full, ((0, 0), (0, 8), (0, 0))).astype(BF16)
    kv_s_new = kv_s[MP:].reshape(DEC_BATCH, DEC_SEQ, kvw)
    new_pad = jnp.pad(kv_s_new, ((0, 0), (0, 16 - DEC_SEQ), (0, 0))).astype(BF16)
    half = kvw // 2
    gl_s = gate_logits[MP:].reshape(DEC_BATCH, DEC_SEQ, 3, N_HEADS).transpose(0, 2, 3, 1).reshape(DEC_BATCH, 3, N_HEADS * DEC_SEQ)
    sel_pages = cache_sel_kv.reshape(-1, PAGE_SIZE, 2 * A_KV_HEADS, HEAD_DIM).transpose(0, 2, 3, 1)
    ot_a = _nsa_sample(sample_ids, qt_a, ckv_sa[:, 0], ckv_sa[:, 1], win_pad[..., :half], win_pad[..., half:],
                       new_pad[..., :half], new_pad[..., half:],
                       _cols_t(tabs["s_cmp"]), _cols_t(tabs["s_win"]), _cols_t(tabs["s_last"]), _cols_t(tabs["s_new"]),
                       c_far, gl_s, sel_pages)
    o_as = _unblock_o(ot_a, A_KV_HEADS, A_GROUP).astype(BF16)

    o_a = jnp.concatenate([o_ap, o_as], axis=0)
    y = _matmul(o_a, a_w_out[la].astype(BF16), tm=TM, tn=1024, out_dtype=F32, name="a_out_proj")
    row_map = lambda i: (i, 0)
    x1, x1b = _residual_ln(x0, [y], [row_map], None, ln_g[0, 0], ln_b[0, 0], tm=192, name="ln_0a")

    hmid = _matmul_swiglu(x1b, dense_w_in[0].astype(BF16), tm=TM, tn=256, name="dense_up")
    f = _matmul(hmid, dense_w_out[0].astype(BF16), tm=528, tn=512, out_dtype=F32, name="dense_down")
    x2, x2b = _residual_ln(x1, [f], [row_map], None, ln_g[0, 1], ln_b[0, 1], tm=192, name="ln_0b")

    sh = _matmul(x2b, b_w_kv.astype(BF16), tm=TM, tn=1024, out_dtype=F32, name="b_kv_proj")
    q1 = _matmul(x2b, b_w_q[0].astype(BF16), tm=TM, tn=1024, out_dtype=BF16, scale=SCALE, name="b_q_proj")
    shw = 2 * B_KV_HEADS * HEAD_DIM
    sh_p = sh[:MP].reshape(BATCH, SEQ, 2, B_KV_HEADS, HEAD_DIM)
    sh_new = sh[MP:].reshape(DEC_BATCH, DEC_SEQ, shw)
    sh5 = sh_p.astype(BF16).transpose(2, 0, 3, 1, 4)
    bias_swa = _tile_cols(tabs["p_swa"], B_KV_HEADS, B_GROUP)
    sink_cols = jnp.repeat(b_sinks[0], QT).reshape(B_KV_HEADS, 1, B_GROUP * QT)
    ot_bp = _swa_prompt(_tile_q(q1[:MP], B_KV_HEADS, B_GROUP), sh5[0], sh5[1], bias_swa, sink_cols)
    o_bp = _untile_o(ot_bp, B_KV_HEADS, B_GROUP)

    sh_full = jnp.concatenate([state_shared_kv.reshape(DEC_BATCH, B_WINDOW, shw), sh_new], axis=1)
    sh_pad = jnp.pad(sh_full, ((0, 0), (0, 8), (0, 0))).astype(BF16)
    qt_b = _block_q(q1[MP:].reshape(DEC_BATCH, DEC_SEQ, Q_WIDTH), B_KV_HEADS, B_GROUP)
    sink_row = jnp.repeat(b_sinks[0], DEC_SEQ).reshape(1, N_HEADS * DEC_SEQ)
    ot_b = _swa_sample(qt_b, sh_pad[..., :shw // 2], sh_pad[..., shw // 2:], _cols_t(tabs["s_swa"]), sink_row)
    o_bs = _unblock_o(ot_b, B_KV_HEADS, B_GROUP).astype(BF16)
    o_b = jnp.concatenate([o_bp, o_bs], axis=0)
    y = _matmul(o_b, b_w_out[0].astype(BF16), tm=TM, tn=1024, out_dtype=F32, name="b_out_proj")
    x3, _ = _residual_ln(x2, [y], [row_map], None, ln_g[1, 0], ln_b[1, 0], tm=192, name="ln_1a")

    yy, gate = _moe(x3, moe_router_w[0], moe_router_b[0], moe_w_in[0], moe_w_out[0])
    nblk = MT // 192
    x4, _ = _residual_ln(x3, [yy, yy], [row_map, lambda i: (i + nblk, 0)], gate, ln_g[1, 1], ln_b[1, 1], tm=192, name="ln_1b")

    y_prompt = x4[:MP].reshape(BATCH, SEQ, D_MODEL)
    y_sample = x4[MP:].reshape(DEC_BATCH, DEC_SEQ, D_MODEL)
    kv6 = lambda a, bsz, t: a.reshape(1, bsz, t, 2, A_KV_HEADS, HEAD_DIM)
    new_cmp_p = kv6(kv_c[:MP], BATCH, SEQ)
    new_cmp_s = kv6(kv_c[MP:], DEC_BATCH, DEC_SEQ)
    new_sel_p = kv6(kv_s[:MP], BATCH, SEQ)
    new_sel_s = kv6(kv_s[MP:], DEC_BATCH, DEC_SEQ)
    new_win_p = kv6(kv_w[:MP], BATCH, SEQ)[:, :, SEQ - A_WINDOW:]
    new_win_s = kv6(win_full[:, DEC_SEQ:], DEC_BATCH, A_WINDOW)
    new_sh_p = sh_p[:, SEQ - B_WINDOW:]
    new_sh_s = sh_full[:, DEC_SEQ:].reshape(DEC_BATCH, B_WINDOW, 2, B_KV_HEADS, HEAD_DIM)
    return (y_prompt, y_sample, new_cmp_p, new_cmp_s, new_sel_p, new_sel_s, new_win_p, new_win_s, new_sh_p, new_sh_s)
```

```python
import functools
import math

import jax
import jax.numpy as jnp
import numpy as np
from jax import lax
from jax.experimental import pallas as pl
from jax.experimental.pallas import tpu as pltpu

D_MODEL = 4096
BATCH = 4
SEQ = 2048
DEPTH = 2
DEC_BATCH = 32
DEC_SEQ = 8
PAST_LEN = 16384
PAGE_SIZE = 128
N_HEADS = 64
HEAD_DIM = 64
Q_WIDTH = N_HEADS * HEAD_DIM
SCALE = HEAD_DIM ** -0.5
A_KV_HEADS = 4
A_GROUP = N_HEADS // A_KV_HEADS
A_KV_WIDTH = 2 * A_KV_HEADS * HEAD_DIM
CMP_BLOCK = 32
CMP_STRIDE = 16
CMP_HIDDEN = 256
SEL_BLOCK = 64
N_SELECT = 16
N_LOCAL = 2
A_WINDOW = 512
B_KV_HEADS = 8
B_GROUP = N_HEADS // B_KV_HEADS
B_WINDOW = 128
REL_BUCKETS = 32
REL_MAX_DIST = 128
D_FF = 11008
N_EXPERTS = 8
TOP_K = 2
D_FF_EXPERT = 14336
ALPHA = (2 * DEPTH) ** 0.25
LN_EPS = 1e-5
NEG_INF = -1e30
TINY = 1e-30
FORCE = 1e9

LANES = 128
MP = BATCH * SEQ
MS = DEC_BATCH * DEC_SEQ
MT = MP + MS
TM = 1056
QT = 128
N_PAGES = PAST_LEN // PAGE_SIZE
VMEM_LIMIT = 56 * 1024 * 1024

F32 = jnp.float32
BF16 = jnp.bfloat16
HIGHEST = lax.Precision.HIGHEST


def _params(sem, vmem=VMEM_LIMIT):
    return pltpu.CompilerParams(dimension_semantics=sem, vmem_limit_bytes=vmem)


def _silu(x):
    return x * (1.0 / (1.0 + jnp.exp(-x)))


def _nt_dot(a, b):
    return lax.dot_general(a, b, (((1,), (1,)), ((), ())), preferred_element_type=F32)


def _tn_dot(a, b):
    return lax.dot_general(a, b, (((0,), (0,)), ((), ())), preferred_element_type=F32)


def _mm_body(x_ref, w_ref, o_ref, *, scale):
    acc = jnp.dot(x_ref[...], w_ref[...], preferred_element_type=F32)
    if scale != 1.0:
        acc = acc * scale
    o_ref[...] = acc.astype(o_ref.dtype)


def _matmul(x, w, *, tm, tn, out_dtype, scale=1.0, name):
    m, k = x.shape
    n = w.shape[1]
    return pl.pallas_call(
        functools.partial(_mm_body, scale=scale),
        grid=(n // tn, m // tm),
        in_specs=[pl.BlockSpec((tm, k), lambda j, i: (i, 0)),
                  pl.BlockSpec((k, tn), lambda j, i: (0, j))],
        out_specs=pl.BlockSpec((tm, tn), lambda j, i: (i, j)),
        out_shape=jax.ShapeDtypeStruct((m, n), out_dtype),
        compiler_params=_params(("arbitrary", "arbitrary")),
        name=name,
    )(x, w)


def _mm_swiglu_body(x_ref, wg_ref, wu_ref, o_ref):
    x = x_ref[...]
    g = jnp.dot(x, wg_ref[...], preferred_element_type=F32)
    u = jnp.dot(x, wu_ref[...], preferred_element_type=F32)
    o_ref[...] = (_silu(g) * u).astype(o_ref.dtype)


def _matmul_swiglu(x, w_in, *, tm, tn, name):
    m, k = x.shape
    f = w_in.shape[1] // 2
    nb = f // tn
    return pl.pallas_call(
        _mm_swiglu_body,
        grid=(m // tm, nb),
        in_specs=[pl.BlockSpec((tm, k), lambda i, j: (i, 0)),
                  pl.BlockSpec((k, tn), lambda i, j: (0, j)),
                  pl.BlockSpec((k, tn), lambda i, j: (0, j + nb))],
        out_specs=pl.BlockSpec((tm, tn), lambda i, j: (i, j)),
        out_shape=jax.ShapeDtypeStruct((m, f), BF16),
        compiler_params=_params(("arbitrary", "arbitrary")),
        name=name,
    )(x, w_in, w_in)


def _ln_body(*refs, alpha, n_add, gated):
    x_ref = refs[0]
    add_refs = refs[1:1 + n_add]
    pos = 1 + n_add
    gate_ref = refs[pos] if gated else None
    pos += 1 if gated else 0
    g_ref, b_ref, o_ref, ob_ref = refs[pos:pos + 4]
    z = alpha * x_ref[...]
    for a, r in enumerate(add_refs):
        y = r[...]
        if gated:
            y = y * gate_ref[:, a:a + 1]
        z = z + y
    mu = jnp.mean(z, axis=-1, keepdims=True)
    zc = z - mu
    var = jnp.mean(zc * zc, axis=-1, keepdims=True)
    out = zc * lax.rsqrt(var + LN_EPS) * g_ref[...] + b_ref[...]
    o_ref[...] = out
    ob_ref[...] = out.astype(BF16)


def _residual_ln(x, adds, add_maps, gate, g, b, *, tm, name):
    m, d = x.shape
    gated = gate is not None
    in_specs = [pl.BlockSpec((tm, d), lambda i: (i, 0))]
    args = [x]
    for a, mp in zip(adds, add_maps):
        in_specs.append(pl.BlockSpec((tm, d), mp))
        args.append(a)
    if gated:
        in_specs.append(pl.BlockSpec((tm, gate.shape[1]), lambda i: (i, 0)))
        args.append(gate)
    in_specs += [pl.BlockSpec((1, d), lambda i: (0, 0)), pl.BlockSpec((1, d), lambda i: (0, 0))]
    args += [g.reshape(1, d), b.reshape(1, d)]
    return pl.pallas_call(
        functools.partial(_ln_body, alpha=ALPHA, n_add=len(adds), gated=gated),
        grid=(m // tm,),
        in_specs=in_specs,
        out_specs=[pl.BlockSpec((tm, d), lambda i: (i, 0)), pl.BlockSpec((tm, d), lambda i: (i, 0))],
        out_shape=[jax.ShapeDtypeStruct((m, d), F32), jax.ShapeDtypeStruct((m, d), BF16)],
        compiler_params=_params(("arbitrary",)),
        name=name,
    )(*args)


def _bias_body(tab_ref, dist_ref, o_ref, *, lo, hi):
    h = pl.program_id(0)
    d = dist_ref[...]
    n = jnp.maximum(d, 0)
    exact = REL_BUCKETS // 2
    nf = jnp.maximum(n, exact).astype(F32)
    large = exact + (jnp.log(nf * (1.0 / exact)) / math.log(REL_MAX_DIST / exact) * (REL_BUCKETS - exact)).astype(jnp.int32)
    bucket = jnp.where(n < exact, n, jnp.minimum(large, REL_BUCKETS - 1))
    acc = jnp.zeros(d.shape, F32)
    for bk in range(REL_BUCKETS):
        acc = jnp.where(bucket == bk, tab_ref[bk, h], acc)
    valid = (d >= lo) & (d <= hi)
    o_ref[0] = jnp.where(valid, acc, NEG_INF)


def _bias_table(rel_table, dist, lo, hi, name):
    r, c = dist.shape
    return pl.pallas_call(
        functools.partial(_bias_body, lo=lo, hi=hi),
        grid=(N_HEADS,),
        in_specs=[pl.BlockSpec(memory_space=pltpu.SMEM),
                  pl.BlockSpec((r, c), lambda h: (0, 0))],
        out_specs=pl.BlockSpec((1, r, c), lambda h: (h, 0, 0)),
        out_shape=jax.ShapeDtypeStruct((N_HEADS, r, c), F32),
        compiler_params=_params(("arbitrary",)),
        name=name,
    )(rel_table, jnp.asarray(dist, jnp.int32))


BIG = 1 << 30


def _dist_tables():
    t128 = np.arange(QT)[None, :]
    tS = np.arange(SEQ)[None, :]
    t8 = PAST_LEN + np.arange(DEC_SEQ)[:, None]
    d = {}
    d["p_cmp"] = (tS - (np.arange(SEQ // CMP_STRIDE)[:, None] * CMP_STRIDE + CMP_BLOCK - 1), 0, BIG)
    d["p_win"] = (t128 + A_WINDOW - np.arange(A_WINDOW + QT)[:, None], 0, A_WINDOW)
    d["p_sel"] = (np.concatenate([u * QT + t128 - np.arange(QT)[:, None] for u in range(3)], axis=0), 0, BIG)
    d["p_swa"] = (t128 + B_WINDOW - np.arange(B_WINDOW + QT)[:, None], 0, B_WINDOW)
    n_cmp_s = PAST_LEN // CMP_STRIDE
    d["s_cmp"] = (t8 - (np.arange(n_cmp_s)[None, :] * CMP_STRIDE + CMP_BLOCK - 1), 0, BIG)
    d["s_win"] = (t8 - (PAST_LEN - A_WINDOW + np.arange(A_WINDOW + 16)[None, :]), 0, A_WINDOW)
    d["s_last"] = (t8 - (PAST_LEN - PAGE_SIZE + np.arange(PAGE_SIZE)[None, :]), 0, BIG)
    d["s_new"] = (t8 - (PAST_LEN + np.arange(16)[None, :]), 0, BIG)
    d["s_swa"] = (t8 - (PAST_LEN - B_WINDOW + np.arange(B_WINDOW + 16)[None, :]), 0, B_WINDOW)
    return d


CMP_PAIRS = CMP_BLOCK // 2


def _compress_body(pid_ref, pages_ref, wpair_ref, w2_ref, pe_ref, w1_ref, b1_ref, o_ref, buf, sem, *, pc, nch, npg):
    b = pl.program_id(0)
    c = pl.program_id(1)
    step = b * nch + c
    nsteps = pl.num_programs(0) * nch
    slot = step % 2
    m = pc * (PAGE_SIZE // CMP_STRIDE)
    n_chunks = A_KV_WIDTH // LANES

    def copies(bb, cc, sl):
        out = []
        for i in range(pc):
            pg = pid_ref[bb, cc * pc + i]
            for lc in range(n_chunks):
                out.append(pltpu.make_async_copy(pages_ref.at[pg, :, pl.ds(lc * LANES, LANES)],
                                                 buf.at[sl, lc, pl.ds(i * PAGE_SIZE, PAGE_SIZE), :], sem.at[sl]))
        pg = pid_ref[bb, jnp.minimum(cc * pc + pc, npg - 1)]
        for lc in range(n_chunks):
            out.append(pltpu.make_async_copy(pages_ref.at[pg, pl.ds(0, CMP_STRIDE), pl.ds(lc * LANES, LANES)],
                                             buf.at[sl, lc, pl.ds(pc * PAGE_SIZE, CMP_STRIDE), :], sem.at[sl]))
        return out

    @pl.when(step == 0)
    def _():
        for cp in copies(b, c, slot):
            cp.start()

    @pl.when(step + 1 < nsteps)
    def _():
        wrap = c + 1 == nch
        nb = jnp.where(wrap, b + 1, b)
        nc = jnp.where(wrap, 0, c + 1)
        for cp in copies(nb, nc, 1 - slot):
            cp.start()

    for cp in copies(b, c, slot):
        cp.wait()

    for kv in range(2):
        pe_term = jnp.dot(pe_ref[kv], w1_ref[kv], preferred_element_type=F32)[0:1] + b1_ref[kv]
        pe2 = jnp.concatenate([pe_term, pe_term], axis=1)
        acc = jnp.zeros((2 * m, 2 * CMP_HIDDEN), F32)
        for q in range(CMP_PAIRS):
            rows = []
            for j in range(2):
                a0 = buf[slot, 2 * kv + j, pl.ds(2 * q, m, stride=CMP_STRIDE), :]
                a1 = buf[slot, 2 * kv + j, pl.ds(2 * q + 1, m, stride=CMP_STRIDE), :]
                rows.append(jnp.concatenate([a0, a1], axis=1))
            lhs = jnp.concatenate(rows, axis=0).astype(BF16)
            acc = acc + jnp.dot(lhs, wpair_ref[kv, q], preferred_element_type=F32)
        hid = _silu(acc + pe2)
        out = jnp.dot(hid.astype(BF16), w2_ref[kv], preferred_element_type=F32)
        o_ref[0, kv, 0] = out[:m].astype(o_ref.dtype)
        o_ref[0, kv, 1] = out[m:].astype(o_ref.dtype)


def _compress(page_ids, pages, wpair, w2bd, pe8, w1f, b1, *, pc, name):
    nb, npg = page_ids.shape
    nch = npg // pc
    m = pc * (PAGE_SIZE // CMP_STRIDE)
    const = lambda nd: (lambda b, c, pid: (0,) * nd)
    grid_spec = pltpu.PrefetchScalarGridSpec(
        num_scalar_prefetch=1,
        grid=(nb, nch),
        in_specs=[pl.BlockSpec(memory_space=pl.ANY),
                  pl.BlockSpec(wpair.shape, const(4)),
                  pl.BlockSpec(w2bd.shape, const(3)),
                  pl.BlockSpec(pe8.shape, const(3)),
                  pl.BlockSpec(w1f.shape, const(3)),
                  pl.BlockSpec(b1.shape, const(3))],
        out_specs=pl.BlockSpec((1, 2, 2, m, LANES), lambda b, c, pid: (b, 0, 0, c, 0)),
        scratch_shapes=[pltpu.VMEM((2, A_KV_WIDTH // LANES, pc * PAGE_SIZE + CMP_STRIDE, LANES), F32),
                        pltpu.SemaphoreType.DMA((2,))],
    )
    return pl.pallas_call(
        functools.partial(_compress_body, pc=pc, nch=nch, npg=npg),
        grid_spec=grid_spec,
        out_shape=jax.ShapeDtypeStruct((nb, 2, 2, npg * (PAGE_SIZE // CMP_STRIDE), LANES), BF16),
        compiler_params=_params(("arbitrary", "arbitrary")),
        name=name,
    )(page_ids, pages, wpair, w2bd, pe8, w1f, b1)


def _select_mask(slc, tpos, blk, n_sel, top_n):
    avail = (blk * SEL_BLOCK <= tpos) & (blk < n_sel)
    lag = tpos // SEL_BLOCK - blk
    forced = (blk == 0) | ((lag >= 0) & (lag < N_LOCAL))
    slc = jnp.where(forced, FORCE, slc)
    slc = jnp.where(avail, slc, -FORCE)

    taken = jnp.zeros(slc.shape, F32)
    for _ in range(top_n):
        mx = jnp.max(slc, axis=1, keepdims=True)
        first = jnp.min(jnp.where(slc == mx, blk, BIG), axis=1, keepdims=True)
        hit = blk == first
        taken = jnp.where(hit, 1.0, taken)
        slc = jnp.where(hit, -jnp.inf, slc)
    return jnp.where(avail, taken, 0.0)


def _overlap_matrix(n_cmp, n_sel, rows, cols):
    msel = np.zeros((rows, cols), np.float32)
    per = SEL_BLOCK // CMP_STRIDE
    for j in range(n_sel):
        for mm in range(per):
            for r in range(CMP_BLOCK // CMP_STRIDE):
                n = per * j + mm - r
                if 0 <= n < n_cmp:
                    msel[n, j] += 1.0
    return msel


def _select_mask_t(slc, tpos, blk, n_sel, top_n):
    avail = blk * SEL_BLOCK <= tpos
    lag = tpos // SEL_BLOCK - blk
    forced = (blk == 0) | ((lag >= 0) & (lag < N_LOCAL))
    slc = jnp.where(forced, FORCE, slc)
    slc = jnp.where(avail, slc, -FORCE)
    cnt = jnp.zeros(slc.shape, F32)
    for jp in range(n_sel):
        row = slc[jp:jp + 1, :]
        ahead = (row > slc) | ((row == slc) & (blk > jp))
        cnt = cnt + jnp.where(ahead, 1.0, 0.0)
    return jnp.where(avail & (cnt < top_n), 1.0, 0.0)


M_FLOOR = 0.5 * NEG_INF


def _online_update(st, v, m_scr, l_scr, acc_scr):
    m_old = m_scr[...]
    m_new = jnp.maximum(m_old, jnp.max(st, axis=0, keepdims=True))
    a = jnp.exp(m_old - m_new)
    et = jnp.exp(st - m_new)
    l_scr[...] = a * l_scr[...] + jnp.sum(et, axis=0, keepdims=True)
    acc_scr[...] = a * acc_scr[...] + _tn_dot(v, et.astype(BF16))
    m_scr[...] = m_new


def _nsa_prompt_body(qt_ref, kc_ref, vc_ref, ks_ref, vs_ref, kw_ref, vw_ref, bc_ref, bw_ref, bt_ref, gl_ref,
                     mselt_ref, o_ref, sel_scr, m_scr, l_scr, acc_scr):
    i = pl.program_id(2)
    grp = A_GROUP
    n_sel = SEQ // SEL_BLOCK
    qt = qt_ref[0, 0, 0]

    bias_c = jnp.concatenate([bc_ref[0, g] for g in range(grp)], axis=1)
    p = _softmax_cols(jnp.dot(kc_ref[0, 0], qt, preferred_element_type=F32) + bias_c)
    oc = _tn_dot(vc_ref[0, 0], p.astype(BF16))

    score = p[:, 0:QT]
    for g in range(1, grp):
        score = score + p[:, g * QT:(g + 1) * QT]
    slc = jnp.dot(mselt_ref[...], score, precision=HIGHEST, preferred_element_type=F32)[0:n_sel]
    tpos = i * QT + lax.broadcasted_iota(jnp.int32, (n_sel, QT), 1)
    blk = lax.broadcasted_iota(jnp.int32, (n_sel, QT), 0)
    sel_scr[...] = _select_mask_t(slc, tpos, blk, n_sel, N_SELECT)

    m_scr[...] = jnp.full(m_scr.shape, M_FLOOR, F32)
    l_scr[...] = jnp.zeros(l_scr.shape, F32)
    acc_scr[...] = jnp.zeros(acc_scr.shape, F32)
    per_tile = QT // SEL_BLOCK

    def sel_tiles(jp, carry):
        pieces = []
        for w in range(2):
            jj = 2 * jp + w
            start = pl.multiple_of(jj * QT, QT)
            u = jnp.clip(i - jj, 0, 2)
            st = jnp.dot(ks_ref[0, 0, pl.ds(start, QT), :], qt, preferred_element_type=F32) + bt_ref[0, u]
            for hb in range(per_tile):
                r = sel_scr[pl.ds(per_tile * jj + hb, 1), :]
                off = (jnp.concatenate([r] * grp, axis=1) - 1.0) * (-NEG_INF)
                pieces.append(st[hb * SEL_BLOCK:(hb + 1) * SEL_BLOCK] + off)
        start2 = pl.multiple_of(jp * (2 * QT), 2 * QT)
        _online_update(jnp.concatenate(pieces, axis=0), vs_ref[0, 0, pl.ds(start2, 2 * QT), :], m_scr, l_scr, acc_scr)
        return carry

    lax.fori_loop(0, i // 2 + 1, sel_tiles, 0)
    osel = acc_scr[...] / jnp.maximum(l_scr[...], TINY)

    n_t = A_WINDOW // QT + 1
    parts = []
    starts = []
    for u in range(n_t):
        j = i - (n_t - 1) + u
        start = pl.multiple_of(jnp.maximum(j, 0) * QT, QT)
        starts.append(start)
        su = jnp.dot(kw_ref[0, 0, pl.ds(start, QT), :], qt, preferred_element_type=F32) + bw_ref[0, u * QT:(u + 1) * QT, :]
        parts.append(jnp.where(j >= 0, su, NEG_INF))
    pw = _softmax_cols(jnp.concatenate(parts, axis=0)).astype(BF16)
    ow = jnp.zeros((HEAD_DIM, grp * QT), F32)
    for u in range(n_t):
        ow = ow + _tn_dot(vw_ref[0, 0, pl.ds(starts[u], QT), :], pw[u * QT:(u + 1) * QT])

    gates = 1.0 / (1.0 + jnp.exp(-gl_ref[0, 0, 0]))
    o_ref[0, 0, 0] = (gates[0:1] * oc + gates[1:2] * osel + gates[2:3] * ow).astype(o_ref.dtype)


def _nsa_prompt(qt, kc, vc, ks, vs, kw, vw, bias_cmp, bias_win, bias_sel, gate_t):
    nq = SEQ // QT
    grp = A_GROUP
    cols = grp * QT
    n_cmp = SEQ // CMP_STRIDE
    mselt = jnp.asarray(_overlap_matrix(n_cmp - 1, SEQ // SEL_BLOCK, n_cmp, LANES).T.copy())
    kv_spec = lambda n: pl.BlockSpec((1, 1, n, HEAD_DIM), lambda b, k, i: (b, k, 0, 0))
    tile_spec = lambda r: pl.BlockSpec((1, 1, 1, r, cols), lambda b, k, i: (b, k, i, 0, 0))
    n_win = A_WINDOW + QT
    return pl.pallas_call(
        _nsa_prompt_body,
        grid=(BATCH, A_KV_HEADS, nq),
        in_specs=[tile_spec(HEAD_DIM),
                  kv_spec(n_cmp), kv_spec(n_cmp), kv_spec(SEQ), kv_spec(SEQ), kv_spec(SEQ), kv_spec(SEQ),
                  pl.BlockSpec((1, grp, n_cmp, QT), lambda b, k, i: (k, 0, 0, i)),
                  pl.BlockSpec((1, n_win, cols), lambda b, k, i: (k, 0, 0)),
                  pl.BlockSpec((1, 3, QT, cols), lambda b, k, i: (k, 0, 0, 0)),
                  tile_spec(3),
                  pl.BlockSpec((LANES, n_cmp), lambda b, k, i: (0, 0))],
        out_specs=tile_spec(HEAD_DIM),
        out_shape=jax.ShapeDtypeStruct((BATCH, A_KV_HEADS, nq, HEAD_DIM, cols), BF16),
        scratch_shapes=[pltpu.VMEM((SEQ // SEL_BLOCK, QT), F32),
                        pltpu.VMEM((1, cols), F32), pltpu.VMEM((1, cols), F32),
                        pltpu.VMEM((HEAD_DIM, cols), F32)],
        compiler_params=_params(("arbitrary", "arbitrary", "arbitrary")),
        name="nsa_prompt",
    )(qt, kc, vc, ks, vs, kw, vw, bias_cmp, bias_win, bias_sel, gate_t, mselt)


SEL_PAGES = 16
S_COLS = A_KV_HEADS * A_GROUP * DEC_SEQ
H_COLS = A_GROUP * DEC_SEQ
S_NSEL = -(-(PAST_LEN + DEC_SEQ) // SEL_BLOCK)
S_NSEL_PAD = 384


def _softmax_cols(s):
    mx = jnp.maximum(jnp.max(s, axis=0, keepdims=True), M_FLOOR)
    e = jnp.exp(s - mx)
    return e / jnp.maximum(jnp.sum(e, axis=0, keepdims=True), TINY)


def _nsa_sample_body(pid_ref, qt_ref, kc_ref, vc_ref, kw_ref, vw_ref, kn_ref, vn_ref, bc_ref, bw_ref, bl_ref, bn_ref,
                     cf_ref, gl_ref, gsum_ref, msel_ref, gexp_ref, eexp_ref, pages_ref, o_ref,
                     buf, sem, sel_scr, oc_scr, ow_scr, m_scr, l_scr, acc_scr, *, npg):
    b = pl.program_id(0)
    c = pl.program_id(1)
    nch = npg // SEL_PAGES
    step = b * nch + c
    nsteps = pl.num_programs(0) * nch
    slot = step % 2

    def copies(bb, cc, sl):
        return [pltpu.make_async_copy(pages_ref.at[pid_ref[bb, cc * SEL_PAGES + i]], buf.at[sl, i], sem.at[sl])
                for i in range(SEL_PAGES)]

    @pl.when(step == 0)
    def _():
        for cp in copies(b, c, slot):
            cp.start()

    @pl.when(step + 1 < nsteps)
    def _():
        wrap = c + 1 == nch
        for cp in copies(jnp.where(wrap, b + 1, b), jnp.where(wrap, 0, c + 1), 1 - slot):
            cp.start()

    qt = qt_ref[0]

    @pl.when(c == 0)
    def _():
        p = _softmax_cols(jnp.dot(kc_ref[0], qt, preferred_element_type=F32) + bc_ref[...])
        oc_scr[...] = _tn_dot(vc_ref[0], p.astype(BF16))
        score = lax.dot_general(gsum_ref[...], p, (((1,), (1,)), ((), ())), precision=HIGHEST,
                                preferred_element_type=F32)
        slc = jnp.dot(score, msel_ref[...], precision=HIGHEST, preferred_element_type=F32)
        r = lax.broadcasted_iota(jnp.int32, slc.shape, 0)
        tpos = PAST_LEN + r % DEC_SEQ
        blk = lax.broadcasted_iota(jnp.int32, slc.shape, 1)
        sel = _select_mask(slc, tpos, blk, S_NSEL, N_SELECT).astype(BF16)
        sel_scr[...] = _tn_dot(sel, gexp_ref[...])
        pw = _softmax_cols(jnp.dot(kw_ref[0], qt, preferred_element_type=F32) + bw_ref[...])
        ow_scr[...] = _tn_dot(vw_ref[0], pw.astype(BF16))
        m_scr[...] = jnp.full(m_scr.shape, M_FLOOR, F32)
        l_scr[...] = jnp.zeros(l_scr.shape, F32)
        acc_scr[...] = jnp.zeros(acc_scr.shape, F32)

    for cp in copies(b, c, slot):
        cp.wait()

    nk = SEL_PAGES * PAGE_SIZE
    blocks = nk // SEL_BLOCK
    selc = sel_scr[pl.ds(pl.multiple_of(c * blocks, blocks), blocks), :].astype(BF16)
    keymask = jnp.dot(eexp_ref[...], selc, preferred_element_type=F32)
    far = jnp.broadcast_to(cf_ref[...], (PAGE_SIZE, S_COLS))
    last = jnp.where(c == nch - 1, bl_ref[...], far)
    for h in range(A_KV_HEADS):
        cs = slice(h * H_COLS, (h + 1) * H_COLS)
        ds_ = slice(h * HEAD_DIM, (h + 1) * HEAD_DIM)
        kt = jnp.concatenate([buf[slot, i, h] for i in range(SEL_PAGES)], axis=1).astype(BF16)
        vt = jnp.concatenate([buf[slot, i, A_KV_HEADS + h] for i in range(SEL_PAGES)], axis=1).astype(BF16)
        st = _tn_dot(kt, qt[ds_, cs])
        st = jnp.concatenate([st[:nk - PAGE_SIZE] + cf_ref[:, cs], st[nk - PAGE_SIZE:] + last[:, cs]], axis=0)
        st = jnp.where(keymask[:, cs] > 0.5, st, NEG_INF)
        m_old = m_scr[:, cs]
        m_new = jnp.maximum(m_old, jnp.max(st, axis=0, keepdims=True))
        a = jnp.exp(m_old - m_new)
        et = jnp.exp(st - m_new)
        l_scr[:, cs] = a * l_scr[:, cs] + jnp.sum(et, axis=0, keepdims=True)
        acc_scr[ds_, cs] = a * acc_scr[ds_, cs] + jnp.dot(vt, et.astype(BF16), preferred_element_type=F32)
        m_scr[:, cs] = m_new

    @pl.when(c == nch - 1)
    def _():
        sn = jnp.dot(kn_ref[0], qt, preferred_element_type=F32) + bn_ref[...]
        new_blk = PAST_LEN // SEL_BLOCK
        sn = jnp.where(sel_scr[new_blk:new_blk + 1, :] > 0.5, sn, NEG_INF)
        _online_update(sn, vn_ref[0], m_scr, l_scr, acc_scr)
        osel = acc_scr[...] / jnp.maximum(l_scr[...], TINY)
        gates = 1.0 / (1.0 + jnp.exp(-gl_ref[0]))
        o_ref[0] = gates[0:1] * oc_scr[...] + gates[1:2] * osel + gates[2:3] * ow_scr[...]


def _nsa_sample(page_table, qt, kc, vc, kwin, vwin, knew, vnew, b_cmp, b_win, b_last, b_new, c_far, gate_t, pages):
    nch = N_PAGES // SEL_PAGES
    n_cmp = PAST_LEN // CMP_STRIDE
    kwid = A_KV_HEADS * HEAD_DIM
    gsum = np.zeros((A_KV_HEADS * DEC_SEQ, S_COLS), np.float32)
    for k in range(A_KV_HEADS):
        for g in range(A_GROUP):
            for t in range(DEC_SEQ):
                gsum[k * DEC_SEQ + t, (k * A_GROUP + g) * DEC_SEQ + t] = 1.0
    msel = _overlap_matrix(n_cmp - 1, S_NSEL, n_cmp, S_NSEL_PAD)
    nk = SEL_PAGES * PAGE_SIZE
    eexp = np.zeros((nk, nk // SEL_BLOCK), np.float32)
    eexp[np.arange(nk), np.arange(nk) // SEL_BLOCK] = 1.0
    per_b = lambda shape: pl.BlockSpec((1,) + shape, lambda b, c, pid: (b, 0, 0))
    const2 = lambda shape: pl.BlockSpec(shape, lambda b, c, pid: (0, 0))
    nwin = kwin.shape[1]
    grid_spec = pltpu.PrefetchScalarGridSpec(
        num_scalar_prefetch=1,
        grid=(DEC_BATCH, nch),
        in_specs=[per_b((kwid, S_COLS)),
                  per_b((n_cmp, kwid)), per_b((n_cmp, kwid)),
                  per_b((nwin, kwid)), per_b((nwin, kwid)),
                  per_b((16, kwid)), per_b((16, kwid)),
                  const2((n_cmp, S_COLS)), const2((nwin, S_COLS)), const2((PAGE_SIZE, S_COLS)), const2((16, S_COLS)),
                  const2((1, S_COLS)), per_b((3, S_COLS)),
                  const2(gsum.shape), const2(msel.shape), const2(gsum.shape), const2(eexp.shape),
                  pl.BlockSpec(memory_space=pl.ANY)],
        out_specs=pl.BlockSpec((1, kwid, S_COLS), lambda b, c, pid: (b, 0, 0)),
        scratch_shapes=[pltpu.VMEM((2, SEL_PAGES, 2 * A_KV_HEADS, HEAD_DIM, PAGE_SIZE), F32),
                        pltpu.SemaphoreType.DMA((2,)),
                        pltpu.VMEM((S_NSEL_PAD, S_COLS), F32),
                        pltpu.VMEM((kwid, S_COLS), F32), pltpu.VMEM((kwid, S_COLS), F32),
                        pltpu.VMEM((1, S_COLS), F32), pltpu.VMEM((1, S_COLS), F32),
                        pltpu.VMEM((kwid, S_COLS), F32)],
    )
    return pl.pallas_call(
        functools.partial(_nsa_sample_body, npg=N_PAGES),
        grid_spec=grid_spec,
        out_shape=jax.ShapeDtypeStruct((DEC_BATCH, kwid, S_COLS), F32),
        compiler_params=_params(("arbitrary", "arbitrary")),
        name="nsa_sample",
    )(page_table, qt, kc, vc, kwin, vwin, knew, vnew, b_cmp, b_win, b_last, b_new, c_far, gate_t,
      jnp.asarray(gsum), jnp.asarray(msel), jnp.asarray(gsum, BF16), jnp.asarray(eexp, BF16), pages)


def _swa_prompt_body(qt_ref, k_ref, v_ref, bias_ref, sink_ref, o_ref):
    i = pl.program_id(2)
    qt = qt_ref[0, 0, 0]
    parts = []
    starts = []
    for u in range(2):
        j = i - 1 + u
        start = pl.multiple_of(jnp.maximum(j, 0) * QT, QT)
        starts.append(start)
        su = jnp.dot(k_ref[0, 0, pl.ds(start, QT), :], qt, preferred_element_type=F32) + bias_ref[0, u * QT:(u + 1) * QT, :]
        parts.append(jnp.where(j >= 0, su, NEG_INF))
    s = jnp.concatenate(parts, axis=0)
    sink = sink_ref[0]
    mx = jnp.maximum(jnp.max(s, axis=0, keepdims=True), sink)
    e = jnp.exp(s - mx)
    den = jnp.sum(e, axis=0, keepdims=True) + jnp.exp(sink - mx)
    p = (e / jnp.maximum(den, TINY)).astype(BF16)
    o = jnp.zeros((HEAD_DIM, qt.shape[1]), F32)
    for u in range(2):
        o = o + _tn_dot(v_ref[0, 0, pl.ds(starts[u], QT), :], p[u * QT:(u + 1) * QT])
    o_ref[0, 0, 0] = o.astype(o_ref.dtype)


def _swa_prompt(qt, k, v, bias, sink_cols):
    nq = SEQ // QT
    cols = B_GROUP * QT
    kv_spec = pl.BlockSpec((1, 1, SEQ, HEAD_DIM), lambda b, h, i: (b, h, 0, 0))
    tile = pl.BlockSpec((1, 1, 1, HEAD_DIM, cols), lambda b, h, i: (b, h, i, 0, 0))
    return pl.pallas_call(
        _swa_prompt_body,
        grid=(BATCH, B_KV_HEADS, nq),
        in_specs=[tile, kv_spec, kv_spec,
                  pl.BlockSpec((1, 2 * QT, cols), lambda b, h, i: (h, 0, 0)),
                  pl.BlockSpec((1, 1, cols), lambda b, h, i: (h, 0, 0))],
        out_specs=tile,
        out_shape=jax.ShapeDtypeStruct((BATCH, B_KV_HEADS, nq, HEAD_DIM, cols), BF16),
        compiler_params=_params(("arbitrary", "arbitrary", "arbitrary")),
        name="swa_prompt",
    )(qt, k, v, bias, sink_cols)


def _swa_sample_body(qt_ref, k_ref, v_ref, bias_ref, sink_ref, o_ref):
    s = jnp.dot(k_ref[0], qt_ref[0], preferred_element_type=F32) + bias_ref[...]
    sink = sink_ref[...]
    mx = jnp.maximum(jnp.max(s, axis=0, keepdims=True), sink)
    e = jnp.exp(s - mx)
    den = jnp.sum(e, axis=0, keepdims=True) + jnp.exp(sink - mx)
    p = e / jnp.maximum(den, TINY)
    o_ref[0] = _tn_dot(v_ref[0], p.astype(BF16))


def _swa_sample(qt, k, v, bias_t, sink_row):
    nk = k.shape[1]
    w = B_KV_HEADS * HEAD_DIM
    per_b = lambda shape: pl.BlockSpec((1,) + shape, lambda b: (b, 0, 0))
    return pl.pallas_call(
        _swa_sample_body,
        grid=(DEC_BATCH,),
        in_specs=[per_b((w, S_COLS)), per_b((nk, w)), per_b((nk, w)),
                  pl.BlockSpec((nk, S_COLS), lambda b: (0, 0)), pl.BlockSpec((1, S_COLS), lambda b: (0, 0))],
        out_specs=per_b((w, S_COLS)),
        out_shape=jax.ShapeDtypeStruct((DEC_BATCH, w, S_COLS), F32),
        compiler_params=_params(("arbitrary",)),
        name="swa_sample",
    )(qt, k, v, bias_t, sink_row)


MOE_TM = 512
MOE_TILES = -(-(TOP_K * MT + N_EXPERTS * (MOE_TM - 1)) // MOE_TM)
MOE_ROWS = MOE_TILES * MOE_TM
MOE_TN_UP = 512
MOE_TN_DOWN = 1024
MOE_SUB = 512
MOE_KSPLIT = 4
MOE_GATHER_ROWS = 256


def _router_body(x_ref, w_ref, b_ref, o_ref):
    logits = jnp.dot(x_ref[...], w_ref[...], precision=HIGHEST, preferred_element_type=F32) + b_ref[...]
    lane = lax.broadcasted_iota(jnp.int32, logits.shape, 1)
    logits = jnp.where(lane < N_EXPERTS, logits, -jnp.inf)
    v1 = jnp.max(logits, axis=1, keepdims=True)
    i1 = jnp.min(jnp.where(logits == v1, lane, LANES), axis=1, keepdims=True)
    rest = jnp.where(lane == i1, -jnp.inf, logits)
    v2 = jnp.max(rest, axis=1, keepdims=True)
    i2 = jnp.min(jnp.where(rest == v2, lane, LANES), axis=1, keepdims=True)
    e2 = jnp.exp(v2 - v1)
    den = 1.0 + e2
    out = jnp.where(lane == 0, i1.astype(F32),
                    jnp.where(lane == 1, i2.astype(F32),
                              jnp.where(lane == 2, 1.0 / den, jnp.where(lane == 3, e2 / den, 0.0))))
    o_ref[...] = out


def _router(x, w, b, *, tm):
    m, d = x.shape
    wp = jnp.zeros((d, LANES), F32).at[:, :N_EXPERTS].set(w)
    bp = jnp.zeros((1, LANES), F32).at[0, :N_EXPERTS].set(b)
    return pl.pallas_call(
        _router_body,
        grid=(m // tm,),
        in_specs=[pl.BlockSpec((tm, d), lambda i: (i, 0)), pl.BlockSpec((d, LANES), lambda i: (0, 0)),
                  pl.BlockSpec((1, LANES), lambda i: (0, 0))],
        out_specs=pl.BlockSpec((tm, LANES), lambda i: (i, 0)),
        out_shape=jax.ShapeDtypeStruct((m, LANES), F32),
        compiler_params=_params(("arbitrary",)),
        name="moe_router",
    )(x, wp, bp)


def _gather_body(idx_ref, src_ref, o_ref, *scratch, rows, staged):
    if staged:
        stage, sem = scratch
    else:
        (sem,) = scratch
        stage = o_ref
    base = pl.program_id(0) * rows

    def row_copy(r, src_row):
        return pltpu.make_async_copy(src_ref.at[pl.ds(src_row, 1)], stage.at[pl.ds(r, 1)], sem)

    def issue(r, carry):
        row_copy(r, idx_ref[base + r]).start()
        return carry

    def drain(r, carry):
        row_copy(r, 0).wait()
        return carry

    lax.fori_loop(0, rows, issue, 0)
    lax.fori_loop(0, rows, drain, 0)
    if staged:
        o_ref[...] = stage[...].astype(o_ref.dtype)


def _gather_rows(src, idx, *, rows, out_dtype, name):
    n = idx.shape[0]
    w = src.shape[1]
    staged = out_dtype != src.dtype
    scratch = ([pltpu.VMEM((rows, w), src.dtype)] if staged else []) + [pltpu.SemaphoreType.DMA(())]
    grid_spec = pltpu.PrefetchScalarGridSpec(
        num_scalar_prefetch=1,
        grid=(n // rows,),
        in_specs=[pl.BlockSpec(memory_space=pl.ANY)],
        out_specs=pl.BlockSpec((rows, w), lambda i, idx_ref: (i, 0)),
        scratch_shapes=scratch,
    )
    return pl.pallas_call(
        functools.partial(_gather_body, rows=rows, staged=staged),
        grid_spec=grid_spec,
        out_shape=jax.ShapeDtypeStruct((n, w), out_dtype),
        compiler_params=_params(("arbitrary",)),
        name=name,
    )(idx, src)


def _moe_up_body(te_ref, nv_ref, nxt_ref, x_ref, w_hbm, o_ref, wbuf, sem, cnt_ref, wg_b, wu_b, *, nb):
    j = pl.program_id(0)
    i = pl.program_id(1)
    e = te_ref[i]
    prev = te_ref[jnp.maximum(i - 1, 0)]

    def group_copies(ee, jj, sl):
        return [pltpu.make_async_copy(w_hbm.at[ee, :, pl.ds(jj * MOE_TN_UP, MOE_TN_UP)], wbuf.at[sl, 0], sem.at[sl]),
                pltpu.make_async_copy(w_hbm.at[ee, :, pl.ds((jj + nb) * MOE_TN_UP, MOE_TN_UP)], wbuf.at[sl, 1], sem.at[sl])]

    @pl.when((j == 0) & (i == 0))
    def _():
        cnt_ref[0] = 0
        for cp in group_copies(e, j, 0):
            cp.start()

    @pl.when((i == 0) | (e != prev))
    def _():
        slot = cnt_ref[0] % 2
        for cp in group_copies(e, j, slot):
            cp.wait()
        nxt = nxt_ref[i]
        wrap = nxt < 0
        ne = jnp.where(wrap, te_ref[0], nxt)
        nj = jnp.where(wrap, j + 1, j)

        @pl.when(nj < nb)
        def _():
            for cp in group_copies(ne, nj, 1 - slot):
                cp.start()

        wg_b[...] = wbuf[slot, 0].astype(BF16)
        wu_b[...] = wbuf[slot, 1].astype(BF16)
        cnt_ref[0] = cnt_ref[0] + 1

    @pl.when(i < nv_ref[0])
    def _():
        x = x_ref[...]
        g = jnp.dot(x, wg_b[...], preferred_element_type=F32)
        u = jnp.dot(x, wu_b[...], preferred_element_type=F32)
        o_ref[...] = (_silu(g) * u).astype(o_ref.dtype)

    @pl.when(i >= nv_ref[0])
    def _():
        o_ref[...] = jnp.zeros(o_ref.shape, o_ref.dtype)


def _moe_up(tile_expert, n_valid, next_expert, xs, w_in):
    d = xs.shape[1]
    f = w_in.shape[2] // 2
    nb = f // MOE_TN_UP
    grid_spec = pltpu.PrefetchScalarGridSpec(
        num_scalar_prefetch=3,
        grid=(nb, MOE_TILES),
        in_specs=[pl.BlockSpec((MOE_TM, d), lambda j, i, te, nv, nx: (i, 0)),
                  pl.BlockSpec(memory_space=pl.ANY)],
        out_specs=pl.BlockSpec((MOE_TM, MOE_TN_UP), lambda j, i, te, nv, nx: (i, j)),
        scratch_shapes=[pltpu.VMEM((2, 2, d, MOE_TN_UP), F32), pltpu.SemaphoreType.DMA((2,)),
                        pltpu.SMEM((1,), jnp.int32),
                        pltpu.VMEM((d, MOE_TN_UP), BF16), pltpu.VMEM((d, MOE_TN_UP), BF16)],
    )
    return pl.pallas_call(
        functools.partial(_moe_up_body, nb=nb),
        grid_spec=grid_spec,
        out_shape=jax.ShapeDtypeStruct((MOE_ROWS, f), BF16),
        compiler_params=_params(("arbitrary", "arbitrary")),
        name="moe_up",
    )(tile_expert, n_valid, next_expert, xs, w_in)


def _moe_down_body(te_ref, nv_ref, *refs, first):
    if first:
        x_ref, w_ref, o_ref, w_b = refs
        prev_ref = None
    else:
        x_ref, w_ref, prev_ref, o_ref, w_b = refs
    i = pl.program_id(1)
    per = MOE_TM // MOE_SUB
    e = te_ref[i // per]
    prev = te_ref[jnp.maximum(i - 1, 0) // per]

    @pl.when((i == 0) | (e != prev))
    def _():
        w_b[...] = w_ref[0].astype(BF16)

    @pl.when(i // per < nv_ref[0])
    def _():
        acc = jnp.dot(x_ref[...], w_b[...], preferred_element_type=F32)
        o_ref[...] = acc if first else acc + prev_ref[...]

    @pl.when(i // per >= nv_ref[0])
    def _():
        o_ref[...] = jnp.zeros(o_ref.shape, o_ref.dtype)


def _moe_down(tile_expert, n_valid, h, w_out, prev, *, part):
    f = h.shape[1]
    d = w_out.shape[2]
    tk = f // MOE_KSPLIT
    per = MOE_TM // MOE_SUB
    first = prev is None
    in_specs = [pl.BlockSpec((MOE_SUB, tk), lambda j, i, te, nv: (i, part)),
                pl.BlockSpec((1, tk, MOE_TN_DOWN), lambda j, i, te, nv: (te[i // per], part, j))]
    args = [h, w_out]
    aliases = {}
    if not first:
        in_specs.append(pl.BlockSpec((MOE_SUB, MOE_TN_DOWN), lambda j, i, te, nv: (i, j)))
        args.append(prev)
        aliases = {4: 0}
    grid_spec = pltpu.PrefetchScalarGridSpec(
        num_scalar_prefetch=2,
        grid=(d // MOE_TN_DOWN, MOE_TILES * per),
        in_specs=in_specs,
        out_specs=pl.BlockSpec((MOE_SUB, MOE_TN_DOWN), lambda j, i, te, nv: (i, j)),
        scratch_shapes=[pltpu.VMEM((tk, MOE_TN_DOWN), BF16)],
    )
    return pl.pallas_call(
        functools.partial(_moe_down_body, first=first),
        grid_spec=grid_spec,
        out_shape=jax.ShapeDtypeStruct((MOE_ROWS, d), F32),
        input_output_aliases=aliases,
        compiler_params=_params(("arbitrary", "arbitrary")),
        name=f"moe_down_{part}",
    )(tile_expert, n_valid, *args)


def _route(top):
    idx = top[:, 0:TOP_K].astype(jnp.int32)
    gate = top[:, TOP_K:2 * TOP_K]
    flat_e = idx.reshape(-1)
    onehot = (flat_e[:, None] == jnp.arange(N_EXPERTS)[None, :]).astype(jnp.int32)
    csum = jnp.cumsum(onehot, axis=0)
    counts = csum[-1]
    rank = jnp.take_along_axis(csum, flat_e[:, None], axis=1)[:, 0] - 1
    tiles_per = (counts + MOE_TM - 1) // MOE_TM
    tile_end = jnp.cumsum(tiles_per)
    tile_start = tile_end - tiles_per
    pos = tile_start[flat_e] * MOE_TM + rank
    src_tok = jnp.zeros((MOE_ROWS,), jnp.int32).at[pos].set(jnp.arange(TOP_K * MT, dtype=jnp.int32) // TOP_K)
    n_valid = tile_end[-1]
    tile_ids = jnp.arange(MOE_TILES, dtype=jnp.int32)
    tile_expert = jnp.searchsorted(tile_end, jnp.minimum(tile_ids, n_valid - 1), side="right").astype(jnp.int32)
    tile_expert = jnp.minimum(tile_expert, N_EXPERTS - 1)
    after = tile_end[tile_expert]
    next_expert = jnp.where(after < n_valid, tile_expert[jnp.minimum(after, MOE_TILES - 1)], -1).astype(jnp.int32)
    return gate, pos.reshape(MT, TOP_K), src_tok, tile_expert, n_valid.reshape(1).astype(jnp.int32), next_expert


def _moe(x_f32, w_router, b_router, w_in, w_out):
    top = _router(x_f32, w_router, b_router, tm=264)
    gate, pos, src_tok, tile_expert, n_valid, next_expert = _route(top)
    xs = _gather_rows(x_f32, src_tok, rows=MOE_GATHER_ROWS, out_dtype=BF16, name="moe_dispatch")
    h = _moe_up(tile_expert, n_valid, next_expert, xs, w_in)
    y = None
    for part in range(MOE_KSPLIT):
        y = _moe_down(tile_expert, n_valid, h, w_out, y, part=part)
    back = jnp.concatenate([pos[:, 0], pos[:, 1]])
    yy = _gather_rows(y, back, rows=MOE_GATHER_ROWS, out_dtype=F32, name="moe_collect")
    return yy, gate


def _tile_q(q, kvh, grp):
    nq = SEQ // QT
    q6 = q.reshape(BATCH, nq, QT, kvh, grp, HEAD_DIM)
    return q6.transpose(0, 3, 1, 5, 4, 2).reshape(BATCH, kvh, nq, HEAD_DIM, grp * QT)


def _untile_o(ot, kvh, grp):
    nq = SEQ // QT
    o6 = ot.reshape(BATCH, kvh, nq, HEAD_DIM, grp, QT)
    return o6.transpose(0, 2, 5, 1, 4, 3).reshape(MP, kvh * grp * HEAD_DIM)


def _tile_cols(tab, kvh, grp):
    rows = tab.shape[1]
    return tab.reshape(kvh, grp, rows, QT).transpose(0, 2, 1, 3).reshape(kvh, rows, grp * QT)


def _block_q(q, kvh, grp):
    q5 = q.reshape(DEC_BATCH, DEC_SEQ, kvh, grp, HEAD_DIM)
    eye = jnp.eye(kvh, dtype=q.dtype)
    qt = jnp.einsum("btkgd,kl->bldkgt", q5, eye)
    return qt.reshape(DEC_BATCH, kvh * HEAD_DIM, kvh * grp * DEC_SEQ)


def _unblock_o(ot, kvh, grp):
    o6 = ot.reshape(DEC_BATCH, kvh, HEAD_DIM, kvh, grp, DEC_SEQ)
    o = jnp.einsum("bkdkgt->btkgd", o6)
    return o.reshape(MS, kvh * grp * HEAD_DIM)


def _cols_t(tab):
    return tab.transpose(2, 0, 1).reshape(tab.shape[2], N_HEADS * DEC_SEQ)


def kernel(x_prompt, x_sample, cache_cmp_kv, cache_sel_kv, state_win_kv, state_shared_kv, page_table, rel_table, ln_g, ln_b, a_w_in, a_cmp_pe, a_cmp_w1, a_cmp_b1, a_cmp_w2, a_w_out, b_w_kv, b_w_q, b_sinks, b_w_out, dense_w_in, dense_w_out, moe_router_w, moe_router_b, moe_w_in, moe_w_out):
    n_phys = cache_cmp_kv.shape[1]
    x0 = jnp.concatenate([x_prompt.reshape(MP, D_MODEL), x_sample.reshape(MS, D_MODEL)], axis=0)
    x0b = x0.astype(BF16)

    dist = _dist_tables()
    tabs = {k: _bias_table(rel_table, d, lo, hi, "bias_" + k) for k, (d, lo, hi) in dist.items()}
    c_far = jnp.repeat(rel_table[REL_BUCKETS - 1], DEC_SEQ).reshape(1, N_HEADS * DEC_SEQ)

    la = 0
    w_in = a_w_in[la]
    kvg_w = jnp.pad(w_in[:, Q_WIDTH:], ((0, 0), (0, 64))).astype(BF16)
    q = _matmul(x0b, w_in[:, :Q_WIDTH].astype(BF16), tm=TM, tn=1024, out_dtype=BF16, scale=SCALE, name="a_q_proj")
    hk = _matmul(x0b, kvg_w, tm=TM, tn=896, out_dtype=F32, name="a_kv_proj")
    kvw = A_KV_WIDTH
    kv_c, kv_s, kv_w = hk[:, 0:kvw], hk[:, kvw:2 * kvw], hk[:, 2 * kvw:3 * kvw]
    gate_logits = hk[:, 3 * kvw:3 * kvw + 3 * N_HEADS]

    w1 = a_cmp_w1[la]
    eye2 = jnp.eye(2, dtype=F32)
    wpair = jnp.einsum("cqpdh,kl->cqpkdlh", w1.reshape(2, CMP_PAIRS, 2, HEAD_DIM, CMP_HIDDEN), eye2)
    wpair = wpair.reshape(2, CMP_PAIRS, 4 * HEAD_DIM, 2 * CMP_HIDDEN).astype(BF16)
    w2bd = jnp.einsum("chd,kl->ckhld", a_cmp_w2[la], eye2).reshape(2, 2 * CMP_HIDDEN, 2 * HEAD_DIM).astype(BF16)
    pe8 = jnp.broadcast_to(a_cmp_pe[la].reshape(2, 1, CMP_BLOCK * HEAD_DIM), (2, 8, CMP_BLOCK * HEAD_DIM)).astype(BF16)
    w1f = w1.reshape(2, CMP_BLOCK * HEAD_DIM, CMP_HIDDEN).astype(BF16)
    b1 = a_cmp_b1[la].reshape(2, 1, CMP_HIDDEN)

    pages_per_seq = SEQ // PAGE_SIZE
    prompt_ids = jnp.arange(BATCH * pages_per_seq, dtype=jnp.int32).reshape(BATCH, pages_per_seq)
    ckv_p = _compress(prompt_ids, kv_c[:MP].reshape(BATCH * pages_per_seq, PAGE_SIZE, kvw), wpair, w2bd, pe8, w1f, b1,
                      pc=pages_per_seq, name="cmp_prompt")
    sample_ids = la * n_phys + page_table
    cmp_pages = lax.optimization_barrier(cache_cmp_kv.reshape(-1, PAGE_SIZE, kvw))
    cmp_pages = lax.optimization_barrier(cmp_pages.reshape(-1, PAGE_SIZE // CMP_STRIDE, CMP_STRIDE, kvw))
    ckv_s = _compress(sample_ids, cmp_pages.reshape(-1, PAGE_SIZE, kvw), wpair, w2bd, pe8, w1f, b1,
                      pc=32, name="cmp_sample")

    n_cmp_p = SEQ // CMP_STRIDE
    ckv_ph = ckv_p.reshape(BATCH, 2, 2, n_cmp_p, 2, HEAD_DIM).transpose(0, 1, 2, 4, 3, 5).reshape(BATCH, 2, A_KV_HEADS, n_cmp_p, HEAD_DIM)

    def kv_heads(kv):
        kv5 = kv.reshape(BATCH, SEQ, 2, A_KV_HEADS, HEAD_DIM).astype(BF16).transpose(2, 0, 3, 1, 4)
        return kv5[0], kv5[1]

    ks_p, vs_p = kv_heads(kv_s[:MP])
    kw_p, vw_p = kv_heads(kv_w[:MP])
    grp = A_GROUP
    nq = SEQ // QT
    bias_cmp = tabs["p_cmp"].reshape(A_KV_HEADS, grp, n_cmp_p, SEQ)
    bias_win = _tile_cols(tabs["p_win"], A_KV_HEADS, grp)
    bias_sel = _tile_cols(tabs["p_sel"], A_KV_HEADS, grp).reshape(A_KV_HEADS, 3, QT, grp * QT)
    gl_p = gate_logits[:MP].reshape(BATCH, nq, QT, 3, A_KV_HEADS, grp).transpose(0, 4, 1, 3, 5, 2).reshape(BATCH, A_KV_HEADS, nq, 3, grp * QT)
    ot_p = _nsa_prompt(_tile_q(q[:MP], A_KV_HEADS, grp), ckv_ph[:, 0], ckv_ph[:, 1], ks_p, vs_p, kw_p, vw_p,
                       bias_cmp, bias_win, bias_sel, gl_p)
    o_ap = _untile_o(ot_p, A_KV_HEADS, grp)

    q_s = q[MP:].reshape(DEC_BATCH, DEC_SEQ, Q_WIDTH)
    qt_a = _block_q(q_s, A_KV_HEADS, A_GROUP)
    n_cmp_s = PAST_LEN // CMP_STRIDE
    ckv_sa = ckv_s.reshape(DEC_BATCH, 2, 2, n_cmp_s, 2 * HEAD_DIM).transpose(0, 1, 3, 2, 4).reshape(DEC_BATCH, 2, n_cmp_s, 4 * HEAD_DIM)
    kv_w_new = kv_w[MP:].reshape(DEC_BATCH, DEC_SEQ, kvw)
    win_full = jnp.concatenate([state_win_kv[la].reshape(DEC_BATCH, A_WINDOW, kvw), kv_w_new], axis=1)
    win_pad = jnp.pad(win_full, ((0, 0), (0, 8), (0, 0))).astype(BF16)
    kv_s_new = kv_s[MP:].reshape(DEC_BATCH, DEC_SEQ, kvw)
    new_pad = jnp.pad(kv_s_new, ((0, 0), (0, 16 - DEC_SEQ), (0, 0))).astype(BF16)
    half = kvw // 2
    gl_s = gate_logits[MP:].reshape(DEC_BATCH, DEC_SEQ, 3, N_HEADS).transpose(0, 2, 3, 1).reshape(DEC_BATCH, 3, N_HEADS * DEC_SEQ)
    sel_pages = cache_sel_kv.reshape(-1, PAGE_SIZE, 2 * A_KV_HEADS, HEAD_DIM).transpose(0, 2, 3, 1)
    ot_a = _nsa_sample(sample_ids, qt_a, ckv_sa[:, 0], ckv_sa[:, 1], win_pad[..., :half], win_pad[..., half:],
                       new_pad[..., :half], new_pad[..., half:],
                       _cols_t(tabs["s_cmp"]), _cols_t(tabs["s_win"]), _cols_t(tabs["s_last"]), _cols_t(tabs["s_new"]),
                       c_far, gl_s, sel_pages)
    o_as = _unblock_o(ot_a, A_KV_HEADS, A_GROUP).astype(BF16)

    o_a = jnp.concatenate([o_ap, o_as], axis=0)
    y = _matmul(o_a, a_w_out[la].astype(BF16), tm=TM, tn=1024, out_dtype=F32, name="a_out_proj")
    row_map = lambda i: (i, 0)
    x1, x1b = _residual_ln(x0, [y], [row_map], None, ln_g[0, 0], ln_b[0, 0], tm=192, name="ln_0a")

    hmid = _matmul_swiglu(x1b, dense_w_in[0].astype(BF16), tm=TM, tn=256, name="dense_up")
    f = _matmul(hmid, dense_w_out[0].astype(BF16), tm=528, tn=512, out_dtype=F32, name="dense_down")
    x2, x2b = _residual_ln(x1, [f], [row_map], None, ln_g[0, 1], ln_b[0, 1], tm=192, name="ln_0b")

    sh = _matmul(x2b, b_w_kv.astype(BF16), tm=TM, tn=1024, out_dtype=F32, name="b_kv_proj")
    q1 = _matmul(x2b, b_w_q[0].astype(BF16), tm=TM, tn=1024, out_dtype=BF16, scale=SCALE, name="b_q_proj")
    shw = 2 * B_KV_HEADS * HEAD_DIM
    sh_p = sh[:MP].reshape(BATCH, SEQ, 2, B_KV_HEADS, HEAD_DIM)
    sh_new = sh[MP:].reshape(DEC_BATCH, DEC_SEQ, shw)
    sh5 = sh_p.astype(BF16).transpose(2, 0, 3, 1, 4)
    bias_swa = _tile_cols(tabs["p_swa"], B_KV_HEADS, B_GROUP)
    sink_cols = jnp.repeat(b_sinks[0], QT).reshape(B_KV_HEADS, 1, B_GROUP * QT)
    ot_bp = _swa_prompt(_tile_q(q1[:MP], B_KV_HEADS, B_GROUP), sh5[0], sh5[1], bias_swa, sink_cols)
    o_bp = _untile_o(ot_bp, B_KV_HEADS, B_GROUP)

    sh_full = jnp.concatenate([state_shared_kv.reshape(DEC_BATCH, B_WINDOW, shw), sh_new], axis=1)
    sh_pad = jnp.pad(sh_full, ((0, 0), (0, 8), (0, 0))).astype(BF16)
    qt_b = _block_q(q1[MP:].reshape(DEC_BATCH, DEC_SEQ, Q_WIDTH), B_KV_HEADS, B_GROUP)
    sink_row = jnp.repeat(b_sinks[0], DEC_SEQ).reshape(1, N_HEADS * DEC_SEQ)
    ot_b = _swa_sample(qt_b, sh_pad[..., :shw // 2], sh_pad[..., shw // 2:], _cols_t(tabs["s_swa"]), sink_row)
    o_bs = _unblock_o(ot_b, B_KV_HEADS, B_GROUP).astype(BF16)
    o_b = jnp.concatenate([o_bp, o_bs], axis=0)
    y = _matmul(o_b, b_w_out[0].astype(BF16), tm=TM, tn=1024, out_dtype=F32, name="b_out_proj")
    x3, _ = _residual_ln(x2, [y], [row_map], None, ln_g[1, 0], ln_b[1, 0], tm=192, name="ln_1a")

    yy, gate = _moe(x3, moe_router_w[0], moe_router_b[0], moe_w_in[0], moe_w_out[0])
    nblk = MT // 192
    x4, _ = _residual_ln(x3, [yy, yy], [row_map, lambda i: (i + nblk, 0)], gate, ln_g[1, 1], ln_b[1, 1], tm=192, name="ln_1b")

    y_prompt = x4[:MP].reshape(BATCH, SEQ, D_MODEL)
    y_sample = x4[MP:].reshape(DEC_BATCH, DEC_SEQ, D_MODEL)
    kv6 = lambda a, bsz, t: a.reshape(1, bsz, t, 2, A_KV_HEADS, HEAD_DIM)
    new_cmp_p = kv6(kv_c[:MP], BATCH, SEQ)
    new_cmp_s = kv6(kv_c[MP:], DEC_BATCH, DEC_SEQ)
    new_sel_p = kv6(kv_s[:MP], BATCH, SEQ)
    new_sel_s = kv6(kv_s[MP:], DEC_BATCH, DEC_SEQ)
    new_win_p = kv6(kv_w[:MP], BATCH, SEQ)[:, :, SEQ - A_WINDOW:]
    new_win_s = kv6(win_full[:, DEC_SEQ:], DEC_BATCH, A_WINDOW)
    new_sh_p = sh_p[:, SEQ - B_WINDOW:]
    new_sh_s = sh_full[:, DEC_SEQ:].reshape(DEC_BATCH, B_WINDOW, 2, B_KV_HEADS, HEAD_DIM)
    return (y_prompt, y_sample, new_cmp_p, new_cmp_s, new_sel_p, new_sel_s, new_win_p, new_win_s, new_sh_p, new_sh_s)
```
